```python
import jax, jax.numpy as jnp
from jax import lax
import numpy as np

D_MODEL = 2048
BATCH = 16
SEQ = 2048
DEPTH = 4

N_MEM = 256
MIX_WIDTH = D_MODEL
HGRN_WIDTH = MIX_WIDTH // 2
CONV_WIDTH = MIX_WIDTH - HGRN_WIDTH
HGRN_HEAD_DIM = 128
HGRN_HEADS = HGRN_WIDTH // HGRN_HEAD_DIM
CONV_GROUP_DIM = 128
CONV_K = 3
CHUNK = 64
XATTN_HEADS = 4
XATTN_HEAD_DIM = D_MODEL // XATTN_HEADS
D_FF = 4 * D_MODEL
EPS = 1e-6
IN_WIDTH = 4 * HGRN_WIDTH + 3 * CONV_WIDTH
SPLITS = [int(s) for s in np.cumsum([HGRN_WIDTH] * 4 + [CONV_WIDTH] * 2)]

kernel_name = "hymba_hgrn2_shortconv_memxattn_trunk"


def rms_norm(x, g):
    xf = x.astype(jnp.float32)
    y = xf * lax.rsqrt(jnp.mean(jnp.square(xf), axis=-1, keepdims=True) + EPS)
    return (y * g.astype(jnp.float32)).astype(x.dtype)


def group_rms_norm(x, g, group):
    shp = x.shape
    xf = x.astype(jnp.float32).reshape(*shp[:-1], shp[-1] // group, group)
    y = xf * lax.rsqrt(jnp.mean(jnp.square(xf), axis=-1, keepdims=True) + EPS)
    return (y.reshape(shp) * g.astype(jnp.float32)).astype(x.dtype)


def hgrn2_mix(q_in, f_in, i_in, lb):
    bsz, seq, _ = q_in.shape
    n_chunks = seq // CHUNK
    f32 = jnp.float32

    def heads(t):
        t = t.astype(f32).reshape(bsz, n_chunks, CHUNK, HGRN_HEADS, HGRN_HEAD_DIM)
        return t.transpose(1, 0, 3, 2, 4)

    f = lb + (1.0 - lb) * jax.nn.sigmoid(f_in.astype(f32))
    q = heads(jax.nn.silu(q_in.astype(f32)))
    k = heads(1.0 - f)
    logf = heads(jnp.log(f))
    v = heads(i_in)
    causal = jnp.tril(jnp.ones((CHUNK, CHUNK), dtype=bool))

    def step(state, inp):
        qc, kc, vc, gc = inp
        b = jnp.cumsum(gc, axis=2)
        o_inter = jnp.einsum('bhck,bhkv->bhcv', qc * jnp.exp(b), state)
        rel = b[:, :, :, None, :] - b[:, :, None, :, :]
        decay = jnp.where(causal[:, :, None], jnp.exp(jnp.minimum(rel, 0.0)), 0.0)
        scores = jnp.einsum('bhtk,bhtsk,bhsk->bhts', qc, decay, kc)
        o = o_inter + jnp.einsum('bhts,bhsv->bhtv', scores, vc)
        b_last = b[:, :, -1:, :]
        state = (jnp.exp(b_last[:, :, 0, :])[..., None] * state
                 + jnp.einsum('bhck,bhcv->bhkv', kc * jnp.exp(b_last - b), vc))
        return state, o

    s0 = jnp.zeros((bsz, HGRN_HEADS, HGRN_HEAD_DIM, HGRN_HEAD_DIM), f32)
    _, o = lax.scan(step, s0, (q, k, v, logf))
    return o.transpose(1, 0, 3, 2, 4).reshape(bsz, seq, HGRN_WIDTH)


def causal_depthwise_conv(u, w):
    return lax.conv_general_dilated(
        u, w[:, None, :].astype(u.dtype), window_strides=(1,),
        padding=[(CONV_K - 1, 0)], dimension_numbers=('NWC', 'WIO', 'NWC'),
        feature_group_count=u.shape[-1])


def setup_inputs(seed: int = 0) -> dict:
    key = jax.random.key(seed)
    ks = jax.random.split(key, 24)
    f32 = jnp.float32

    def nrm(k, shape, scale):
        return jax.random.normal(k, shape, f32) * scale

    def gain(k, dim):
        return 1.0 + 0.02 * jax.random.normal(k, (DEPTH, dim), f32)

    return {
        "x": nrm(ks[0], (BATCH, SEQ, D_MODEL), 1.0),
        "mem": nrm(ks[1], (BATCH, N_MEM, D_MODEL), 1.0),
        "g_mix_pre": gain(ks[2], D_MODEL),
        "w_in": nrm(ks[3], (DEPTH, D_MODEL, IN_WIDTH), D_MODEL ** -0.5),
        "hgrn_lb_logits": nrm(ks[4], (DEPTH, HGRN_WIDTH), 1.0),
        "hgrn_norm_g": gain(ks[5], HGRN_WIDTH),
        "conv_w": nrm(ks[6], (DEPTH, CONV_K, CONV_WIDTH), CONV_K ** -0.5),
        "conv_norm_g": gain(ks[7], CONV_WIDTH),
        "w_mix_out": nrm(ks[8], (DEPTH, MIX_WIDTH, D_MODEL), MIX_WIDTH ** -0.5),
        "g_mix_post": gain(ks[9], D_MODEL),
        "g_x_pre": gain(ks[10], D_MODEL),
        "g_mem": gain(ks[11], D_MODEL),
        "w_q": nrm(ks[12], (DEPTH, D_MODEL, D_MODEL), D_MODEL ** -0.5),
        "w_k": nrm(ks[13], (DEPTH, D_MODEL, D_MODEL), D_MODEL ** -0.5),
        "w_v": nrm(ks[14], (DEPTH, D_MODEL, D_MODEL), D_MODEL ** -0.5),
        "w_xo": nrm(ks[15], (DEPTH, D_MODEL, D_MODEL), D_MODEL ** -0.5),
        "g_x_post": gain(ks[16], D_MODEL),
        "g_mlp_pre": gain(ks[17], D_MODEL),
        "w_up": nrm(ks[18], (DEPTH, D_MODEL, D_FF), D_MODEL ** -0.5),
        "w_down": nrm(ks[19], (DEPTH, D_FF, D_MODEL), D_FF ** -0.5),
        "g_mlp_post": gain(ks[20], D_MODEL),
    }


def reference(x, mem, g_mix_pre, w_in, hgrn_lb_logits, hgrn_norm_g, conv_w, conv_norm_g,
              w_mix_out, g_mix_post, g_x_pre, g_mem, w_q, w_k, w_v, w_xo, g_x_post,
              g_mlp_pre, w_up, w_down, g_mlp_post):
    bsz, seq, _ = x.shape
    n_mem = mem.shape[1]
    lb_cum = jnp.cumsum(jax.nn.softmax(hgrn_lb_logits.astype(jnp.float32), axis=0), axis=0)
    lower_bounds = lb_cum - lb_cum[0:1]

    for l in range(DEPTH):
        h = rms_norm(x, g_mix_pre[l])
        proj = h @ w_in[l]
        q_in, f_in, i_in, g_in, c_b, c_c, c_h = jnp.split(proj, SPLITS, axis=-1)
        o_h = hgrn2_mix(q_in, f_in, i_in, lower_bounds[l]).astype(x.dtype)
        o_h = group_rms_norm(o_h, hgrn_norm_g[l], HGRN_HEAD_DIM) * jax.nn.silu(g_in)
        y_c = causal_depthwise_conv(c_c * c_h, conv_w[l])
        o_c = group_rms_norm(c_b * y_c, conv_norm_g[l], CONV_GROUP_DIM)
        mix = jnp.concatenate([o_h, o_c], axis=-1) @ w_mix_out[l]
        x = x + rms_norm(mix, g_mix_post[l])

        h = rms_norm(x, g_x_pre[l])
        m = rms_norm(mem, g_mem[l])
        q = (h @ w_q[l]).reshape(bsz, seq, XATTN_HEADS, XATTN_HEAD_DIM)
        k = (m @ w_k[l]).reshape(bsz, n_mem, XATTN_HEADS, XATTN_HEAD_DIM)
        v = (m @ w_v[l]).reshape(bsz, n_mem, XATTN_HEADS, XATTN_HEAD_DIM)
        s = jnp.einsum('bshd,bnhd->bhsn', q, k).astype(jnp.float32) * (XATTN_HEAD_DIM ** -0.5)
        p = jax.nn.softmax(s, axis=-1).astype(x.dtype)
        a = jnp.einsum('bhsn,bnhd->bshd', p, v).reshape(bsz, seq, D_MODEL) @ w_xo[l]
        x = x + rms_norm(a, g_x_post[l])

        h = rms_norm(x, g_mlp_pre[l])
        u = jnp.square(jax.nn.relu(h @ w_up[l])) @ w_down[l]
        x = x + rms_norm(u, g_mlp_post[l])
    return x
```

```python
import functools

import numpy as np
import jax
import jax.numpy as jnp
from jax import lax
from jax.experimental import pallas as pl
from jax.experimental.pallas import tpu as pltpu

F32 = jnp.float32
BF16 = jnp.bfloat16

EPS = 1e-6
HEAD_DIM = 128
CHUNK = 64
DIAG = 16
LEVELS = tuple(h for h in (32, 16, 8) if h >= DIAG)
CONV_K = 3
XATTN_HEADS = 4
V7X_VMEM_LIMIT_BYTES = 56 * 1024 * 1024

NT_DIMS = (((1,), (1,)), ((), ()))
TN_DIMS = (((0,), (0,)), ((), ()))


def _params(*semantics):
    return pltpu.CompilerParams(dimension_semantics=semantics,
                                vmem_limit_bytes=V7X_VMEM_LIMIT_BYTES)


def _rms(x, g):
    return x * lax.rsqrt(jnp.mean(x * x, axis=-1, keepdims=True) + EPS) * g


def _sigmoid(x):
    return 1.0 / (1.0 + jnp.exp(-x))


def _resident(shape):
    return pl.BlockSpec(shape, lambda *_: (0,) * len(shape), pipeline_mode=pl.Buffered(1))


def _inproj_kernel(x_ref, g_ref, w_ref, o_ref, hn_ref):
    @pl.when(pl.program_id(1) == 0)
    def _():
        hn_ref[...] = _rms(x_ref[...], g_ref[...]).astype(BF16)

    o_ref[...] = jnp.dot(hn_ref[...], w_ref[...], preferred_element_type=F32).astype(o_ref.dtype)


def _inproj(x2, g, w, *, tm, tn):
    m, d = x2.shape
    n = w.shape[1]
    return pl.pallas_call(
        _inproj_kernel,
        out_shape=jax.ShapeDtypeStruct((m, n), F32),
        grid=(m // tm, n // tn),
        in_specs=[pl.BlockSpec((tm, d), lambda i, j: (i, 0)),
                  pl.BlockSpec((1, d), lambda i, j: (0, 0)),
                  pl.BlockSpec((d, tn), lambda i, j: (0, j))],
        out_specs=pl.BlockSpec((tm, tn), lambda i, j: (i, j)),
        scratch_shapes=[pltpu.VMEM((tm, d), BF16)],
        compiler_params=_params("arbitrary", "arbitrary"),
        name="inproj",
    )(x2, g, w)


def _hgrn_constants():
    t = np.arange(CHUNK)
    tri = (t[None, :] <= t[:, None])
    ghat = (t[None, :] > t[:, None])
    gloc = tri & (t[None, :] // DIAG == t[:, None] // DIAG)
    mats = [tri, ghat, gloc]
    rsel, lmask = [], []
    for h in LEVELS:
        blk = t // h
        odd = (blk % 2 == 1)
        g = np.zeros((CHUNK, CHUNK), bool)
        for i in range(CHUNK):
            if odd[i]:
                g[i, blk[i] * h:i + 1] = True
            else:
                g[i, i + 1:(blk[i] + 1) * h] = True
        mats.append(g)
        rsel.append(np.broadcast_to(odd[:, None], (CHUNK, HEAD_DIM)))
        lmask.append(odd[:, None] & (blk[None, :] == blk[:, None] - 1))
    gstack = np.concatenate(mats, axis=0).astype(np.float32)
    gstack = np.concatenate([gstack] * 3, axis=1)
    col = np.arange(CHUNK)
    emat = (np.arange(DIAG * HEAD_DIM)[:, None] // HEAD_DIM == col[None, :] % DIAG)
    dmask = (col[None, :] // DIAG == t[:, None] // DIAG)
    return (jnp.asarray(gstack, BF16), jnp.asarray(np.stack(rsel), F32),
            jnp.asarray(np.stack(lmask), F32), jnp.asarray(emat, BF16), jnp.asarray(dmask, F32))


def _hgrn_kernel(layer, q_ref, f_ref, i_ref, g_ref, cb_ref, cc_ref, ch_ref,
                 lbl_ref, gh_ref, cw_ref, gc_ref,
                 gstack_ref, rsel_ref, lmask_ref, emat_ref, dmask_ref,
                 oh_ref, oc_ref, st_ref, carry_ref):
    rows_t = q_ref.shape[0]

    @pl.when(pl.program_id(2) == 0)
    def _():
        st_ref[...] = jnp.zeros_like(st_ref)
        carry_ref[...] = jnp.zeros_like(carry_ref)

    lg = [lbl_ref[l:l + 1, :] for l in range(lbl_ref.shape[0])]
    mx = functools.reduce(jnp.maximum, lg)
    ex = [jnp.exp(v - mx) for v in lg]
    den = functools.reduce(lambda a, b: a + b, ex)
    lb = jnp.zeros_like(mx)
    for l in range(1, layer + 1):
        lb = lb + ex[l] / den
    gh = gh_ref[...]
    nlev = len(LEVELS)

    def chunk(c, _):
        r0 = pl.multiple_of(c * CHUNK, CHUNK)
        rows = pl.ds(r0, CHUNK)
        qin = q_ref[rows, :]
        fin = f_ref[rows, :]
        vin = i_ref[rows, :]
        gin = g_ref[rows, :]
        f = lb + (1.0 - lb) * _sigmoid(fin)
        logf = jnp.log(f)
        kk = 1.0 - f
        qs = qin * _sigmoid(qin)
        vb = vin.astype(BF16)

        hi = logf.astype(BF16)
        r1 = logf - hi.astype(F32)
        mid = r1.astype(BF16)
        lo = (r1 - mid.astype(F32)).astype(BF16)
        xs = jnp.dot(gstack_ref[...], jnp.concatenate([hi, mid, lo], axis=0),
                     preferred_element_type=F32)
        b = xs[0:CHUNK]
        bhat = xs[CHUNK:2 * CHUNK]
        bl = xs[2 * CHUNK:3 * CHUNK]

        st = st_ref[...]
        qe = (qs * jnp.exp(b)).astype(BF16)
        o = lax.dot_general(qe, st.astype(BF16), NT_DIMS, preferred_element_type=F32)
        khat = (kk * jnp.exp(bhat)).astype(BF16)
        upd = lax.dot_general(vb, khat, TN_DIMS, preferred_element_type=F32)
        st_ref[...] = st * jnp.exp(b[CHUNK - 1:CHUNK, :]) + upd

        a = jnp.zeros((CHUNK, CHUNK), F32)
        for li in range(nlev):
            xl = xs[(3 + li) * CHUNK:(4 + li) * CHUNK]
            w = (jnp.where(rsel_ref[li] > 0.5, qs, kk) * jnp.exp(xl)).astype(BF16)
            a = a + lax.dot_general(w, w, NT_DIMS, preferred_element_type=F32) * lmask_ref[li]

        tl = lax.broadcasted_iota(jnp.int32, (DIAG, HEAD_DIM), 0)
        pieces = []
        for s in range(DIAG):
            blocks = []
            for blk in range(CHUNK // DIAG):
                lo_r = blk * DIAG
                bb = bl[lo_r:lo_r + DIAG]
                dec = jnp.exp(jnp.minimum(bb - bl[lo_r + s:lo_r + s + 1], 0.0))
                p = dec * kk[lo_r + s:lo_r + s + 1] * qs[lo_r:lo_r + DIAG]
                blocks.append(jnp.where(tl >= s, p, 0.0))
            pieces.append(jnp.concatenate(blocks, axis=0).astype(BF16))
        ad = jnp.dot(jnp.concatenate(pieces, axis=1), emat_ref[...], preferred_element_type=F32)
        a = a + ad * dmask_ref[...]
        o = o + jnp.dot(a.astype(BF16), vb, preferred_element_type=F32)

        y = o * lax.rsqrt(jnp.mean(o * o, axis=-1, keepdims=True) + EPS) * gh
        oh_ref[rows, :] = (y * (gin * _sigmoid(gin))).astype(oh_ref.dtype)
        return 0

    lax.fori_loop(0, rows_t // CHUNK, chunk, 0)

    u = cc_ref[...] * ch_ref[...]
    row = lax.broadcasted_iota(jnp.int32, u.shape, 0)
    prev1 = carry_ref[7:8, :]
    prev2 = carry_ref[6:7, :]
    u1 = jnp.where(row == 0, prev1, pltpu.roll(u, 1, axis=0))
    u2 = jnp.where(row == 0, prev2, jnp.where(row == 1, prev1, pltpu.roll(u, 2, axis=0)))
    carry_ref[...] = u[rows_t - 8:rows_t, :]
    yc = cw_ref[0:1, :] * u2 + cw_ref[1:2, :] * u1 + cw_ref[2:3, :] * u
    z = cb_ref[...] * yc
    zn = z * lax.rsqrt(jnp.mean(z * z, axis=-1, keepdims=True) + EPS) * gc_ref[...]
    oc_ref[...] = zn.astype(oc_ref.dtype)


def _hgrn_conv(proj3, lb_logits, gh, cw, gc, layer, consts, *, tt):
    bsz, seq, _ = proj3.shape
    nh = gh.shape[1] // HEAD_DIM

    def col(k):
        return pl.BlockSpec((None, tt, HEAD_DIM), lambda b, h, t, k=k: (b, t, h + nh * k))

    def par(nrows):
        return pl.BlockSpec((nrows, HEAD_DIM), lambda b, h, t: (0, h))

    out_sd = jax.ShapeDtypeStruct((bsz, seq, nh * HEAD_DIM), BF16)
    out_spec = pl.BlockSpec((None, tt, HEAD_DIM), lambda b, h, t: (b, t, h))
    return pl.pallas_call(
        functools.partial(_hgrn_kernel, layer),
        out_shape=(out_sd, out_sd),
        grid=(bsz, nh, seq // tt),
        in_specs=[col(k) for k in range(7)]
        + [par(lb_logits.shape[0]), par(1), par(CONV_K), par(1)]
        + [_resident(c.shape) for c in consts],
        out_specs=(out_spec, out_spec),
        scratch_shapes=[pltpu.VMEM((HEAD_DIM, HEAD_DIM), F32), pltpu.VMEM((8, HEAD_DIM), F32)],
        compiler_params=_params("arbitrary", "arbitrary", "arbitrary"),
        name="hgrn_conv",
    )(*([proj3] * 7), lb_logits, gh, cw, gc, *consts)


def _mixout_kernel(oh_ref, oc_ref, x_ref, w_ref, g_ref, o_ref):
    kh = oh_ref.shape[1]
    mix = jnp.dot(oh_ref[...], w_ref[0:kh, :], preferred_element_type=F32)
    mix = mix + jnp.dot(oc_ref[...], w_ref[kh:, :], preferred_element_type=F32)
    o_ref[...] = x_ref[...] + _rms(mix, g_ref[...])


def _mixout(oh, oc, x2, w, g, *, tm):
    m, d = x2.shape
    kh = oh.shape[1]
    return pl.pallas_call(
        _mixout_kernel,
        out_shape=jax.ShapeDtypeStruct((m, d), F32),
        grid=(m // tm,),
        in_specs=[pl.BlockSpec((tm, kh), lambda i: (i, 0)),
                  pl.BlockSpec((tm, kh), lambda i: (i, 0)),
                  pl.BlockSpec((tm, d), lambda i: (i, 0)),
                  _resident(w.shape), _resident(g.shape)],
        out_specs=pl.BlockSpec((tm, d), lambda i: (i, 0)),
        compiler_params=_params("arbitrary"),
        name="mixout",
    )(oh, oc, x2, w, g)


def _kv_kernel(m_ref, g_ref, w_ref, o_ref):
    mn = _rms(m_ref[...], g_ref[...]).astype(BF16)
    o_ref[...] = jnp.dot(mn, w_ref[...], preferred_element_type=F32).astype(o_ref.dtype)


def _kvproj(mem2, g, wkv, *, tm):
    m, d = mem2.shape
    return pl.pallas_call(
        _kv_kernel,
        out_shape=jax.ShapeDtypeStruct((2, m, d), BF16),
        grid=(2, m // tm),
        in_specs=[pl.BlockSpec((tm, d), lambda s, i: (i, 0)),
                  pl.BlockSpec((1, d), lambda s, i: (0, 0)),
                  pl.BlockSpec((None, d, d), lambda s, i: (s, 0, 0))],
        out_specs=pl.BlockSpec((None, tm, d), lambda s, i: (s, i, 0)),
        compiler_params=_params("arbitrary", "arbitrary"),
        name="kvproj",
    )(mem2, g, wkv)


def _xattn_kernel(x_ref, gpre_ref, wq_ref, k_ref, v_ref, wo_ref, gpost_ref, o_ref):
    x = x_ref[...]
    d = x.shape[1]
    hd = d // XATTN_HEADS
    q = jnp.dot(_rms(x, gpre_ref[...]).astype(BF16), wq_ref[...], preferred_element_type=F32)
    heads = []
    for h in range(XATTN_HEADS):
        cols = slice(h * hd, (h + 1) * hd)
        s = lax.dot_general(q[:, cols].astype(BF16), k_ref[:, cols], NT_DIMS,
                            preferred_element_type=F32) * (hd ** -0.5)
        p = jnp.exp(s - jnp.max(s, axis=-1, keepdims=True))
        p = p / jnp.sum(p, axis=-1, keepdims=True)
        heads.append(jnp.dot(p.astype(BF16), v_ref[:, cols],
                             preferred_element_type=F32).astype(BF16))
    a = jnp.dot(jnp.concatenate(heads, axis=1), wo_ref[...], preferred_element_type=F32)
    o_ref[...] = x + _rms(a, gpost_ref[...])


def _xattn(x3, gpre, wq, kv, wo, gpost, *, tq):
    bsz, seq, d = x3.shape
    nmem = kv.shape[2]
    return pl.pallas_call(
        _xattn_kernel,
        out_shape=jax.ShapeDtypeStruct((bsz, seq, d), F32),
        grid=(bsz, seq // tq),
        in_specs=[pl.BlockSpec((None, tq, d), lambda b, t: (b, t, 0)),
                  _resident(gpre.shape), _resident(wq.shape),
                  pl.BlockSpec((None, None, nmem, d), lambda b, t: (0, b, 0, 0)),
                  pl.BlockSpec((None, None, nmem, d), lambda b, t: (1, b, 0, 0)),
                  _resident(wo.shape), _resident(gpost.shape)],
        out_specs=pl.BlockSpec((None, tq, d), lambda b, t: (b, t, 0)),
        compiler_params=_params("arbitrary", "arbitrary"),
        name="xattn",
    )(x3, gpre, wq, kv, kv, wo, gpost)


def _mlp_kernel(x_ref, gpre_ref, wu_ref, wd_ref, gpost_ref, o_ref, hn_ref, acc_ref):
    j = pl.program_id(1)

    @pl.when(j == 0)
    def _():
        hn_ref[...] = _rms(x_ref[...], gpre_ref[...]).astype(BF16)

    u = jnp.maximum(jnp.dot(hn_ref[...], wu_ref[...], preferred_element_type=F32), 0.0)
    part = jnp.dot((u * u).astype(BF16), wd_ref[...], preferred_element_type=F32)

    @pl.when(j == 0)
    def _():
        acc_ref[...] = part

    @pl.when(j > 0)
    def _():
        acc_ref[...] += part

    @pl.when(j == pl.num_programs(1) - 1)
    def _():
        o_ref[...] = x_ref[...] + _rms(acc_ref[...], gpost_ref[...])


def _mlp(x2, gpre, wu, wd, gpost, *, tm, tf):
    m, d = x2.shape
    ff = wu.shape[1]
    return pl.pallas_call(
        _mlp_kernel,
        out_shape=jax.ShapeDtypeStruct((m, d), F32),
        grid=(m // tm, ff // tf),
        in_specs=[pl.BlockSpec((tm, d), lambda i, j: (i, 0)),
                  _resident(gpre.shape),
                  pl.BlockSpec((d, tf), lambda i, j: (0, j)),
                  pl.BlockSpec((tf, d), lambda i, j: (j, 0)),
                  _resident(gpost.shape)],
        out_specs=pl.BlockSpec((tm, d), lambda i, j: (i, 0)),
        scratch_shapes=[pltpu.VMEM((tm, d), BF16), pltpu.VMEM((tm, d), F32)],
        compiler_params=_params("arbitrary", "arbitrary"),
        name="mlp",
    )(x2, gpre, wu, wd, gpost)


def _tile(n, want):
    t = min(n, want)
    while n % t:
        t //= 2
    return t


def kernel(x, mem, g_mix_pre, w_in, hgrn_lb_logits, hgrn_norm_g, conv_w, conv_norm_g, w_mix_out,
           g_mix_post, g_x_pre, g_mem, w_q, w_k, w_v, w_xo, g_x_post, g_mlp_pre, w_up, w_down,
           g_mlp_post):
    bsz, seq, d = x.shape
    depth = w_in.shape[0]
    nmem = mem.shape[1]
    m = bsz * seq

    w_in_b = w_in.astype(BF16)
    w_mix_b = w_mix_out.astype(BF16)
    w_q_b = w_q.astype(BF16)
    w_kv_b = jnp.stack([w_k, w_v], axis=1).astype(BF16)
    w_xo_b = w_xo.astype(BF16)
    w_up_b = w_up.astype(BF16)
    w_down_b = w_down.astype(BF16)
    consts = _hgrn_constants()

    def row(p, l):
        return p[l][None, :]

    mem2 = mem.reshape(bsz * nmem, d)
    x2 = x.reshape(m, d)
    for l in range(depth):
        proj = _inproj(x2, row(g_mix_pre, l), w_in_b[l], tm=_tile(m, 1024), tn=_tile(w_in.shape[2], 1024))
        oh, oc = _hgrn_conv(proj.reshape(bsz, seq, -1), hgrn_lb_logits, row(hgrn_norm_g, l),
                            conv_w[l], row(conv_norm_g, l), l, consts, tt=_tile(seq, 512))
        x2 = _mixout(oh.reshape(m, -1), oc.reshape(m, -1), x2, w_mix_b[l], row(g_mix_post, l),
                     tm=_tile(m, 512))
        kv = _kvproj(mem2, row(g_mem, l), w_kv_b[l], tm=_tile(bsz * nmem, 1024))
        x3 = _xattn(x2.reshape(bsz, seq, d), row(g_x_pre, l), w_q_b[l],
                    kv.reshape(2, bsz, nmem, d), w_xo_b[l], row(g_x_post, l), tq=_tile(seq, 512))
        x2 = _mlp(x3.reshape(m, d), row(g_mlp_pre, l), w_up_b[l], w_down_b[l], row(g_mlp_post, l),
                  tm=_tile(m, 512), tf=_tile(w_up.shape[2], 1024))
    return x2.reshape(bsz, seq, d)
```

```python
import functools

import numpy as np
import jax
import jax.numpy as jnp
from jax import lax
from jax.experimental import pallas as pl
from jax.experimental.pallas import tpu as pltpu

F32 = jnp.float32
BF16 = jnp.bfloat16

EPS = 1e-6
HEAD_DIM = 128
CHUNK = 64
DIAG = 8
LEVELS = tuple(h for h in (32, 16, 8) if h >= DIAG)
CONV_K = 3
XATTN_HEADS = 4
V7X_VMEM_LIMIT_BYTES = 56 * 1024 * 1024

NT_DIMS = (((1,), (1,)), ((), ()))
TN_DIMS = (((0,), (0,)), ((), ()))


def _params(*semantics):
    return pltpu.CompilerParams(dimension_semantics=semantics,
                                vmem_limit_bytes=V7X_VMEM_LIMIT_BYTES)


def _rms(x, g):
    return x * lax.rsqrt(jnp.mean(x * x, axis=-1, keepdims=True) + EPS) * g


def _sigmoid(x):
    return 1.0 / (1.0 + jnp.exp(-x))


def _resident(shape):
    return pl.BlockSpec(shape, lambda *_: (0,) * len(shape), pipeline_mode=pl.Buffered(1))


def _inproj_kernel(x_ref, g_ref, w_ref, o_ref, hn_ref):
    @pl.when(pl.program_id(1) == 0)
    def _():
        hn_ref[...] = _rms(x_ref[...], g_ref[...]).astype(BF16)

    o_ref[...] = jnp.dot(hn_ref[...], w_ref[...], preferred_element_type=F32).astype(o_ref.dtype)


def _inproj(x2, g, w, *, tm, tn):
    m, d = x2.shape
    n = w.shape[1]
    return pl.pallas_call(
        _inproj_kernel,
        out_shape=jax.ShapeDtypeStruct((m, n), F32),
        grid=(m // tm, n // tn),
        in_specs=[pl.BlockSpec((tm, d), lambda i, j: (i, 0)),
                  pl.BlockSpec((1, d), lambda i, j: (0, 0)),
                  pl.BlockSpec((d, tn), lambda i, j: (0, j))],
        out_specs=pl.BlockSpec((tm, tn), lambda i, j: (i, j)),
        scratch_shapes=[pltpu.VMEM((tm, d), BF16)],
        compiler_params=_params("arbitrary", "arbitrary"),
        name="inproj",
    )(x2, g, w)


def _hgrn_constants():
    t = np.arange(CHUNK)
    tri = (t[None, :] <= t[:, None])
    ghat = (t[None, :] > t[:, None])
    gloc = tri & (t[None, :] // DIAG == t[:, None] // DIAG)
    mats = [tri, ghat, gloc]
    rsel, lmask = [], []
    for h in LEVELS:
        blk = t // h
        odd = (blk % 2 == 1)
        g = np.zeros((CHUNK, CHUNK), bool)
        for i in range(CHUNK):
            if odd[i]:
                g[i, blk[i] * h:i + 1] = True
            else:
                g[i, i + 1:(blk[i] + 1) * h] = True
        mats.append(g)
        rsel.append(np.broadcast_to(odd[:, None], (CHUNK, HEAD_DIM)))
        lmask.append(odd[:, None] & (blk[None, :] == blk[:, None] - 1))
    gstack = np.concatenate(mats, axis=0).astype(np.float32)
    gstack = np.concatenate([gstack] * 3, axis=1)
    col = np.arange(CHUNK)
    emat = (np.arange(DIAG * HEAD_DIM)[:, None] // HEAD_DIM == col[None, :] % DIAG)
    dmask = (col[None, :] // DIAG == t[:, None] // DIAG)
    return (jnp.asarray(gstack, BF16), jnp.asarray(np.stack(rsel), F32),
            jnp.asarray(np.stack(lmask), F32), jnp.asarray(emat, BF16), jnp.asarray(dmask, F32))


def _hgrn_kernel(layer, q_ref, f_ref, i_ref, g_ref, cb_ref, cc_ref, ch_ref,
                 lbl_ref, gh_ref, cw_ref, gc_ref,
                 gstack_ref, rsel_ref, lmask_ref, emat_ref, dmask_ref,
                 oh_ref, oc_ref, st_ref, carry_ref):
    rows_t = q_ref.shape[0]

    @pl.when(pl.program_id(2) == 0)
    def _():
        st_ref[...] = jnp.zeros_like(st_ref)
        carry_ref[...] = jnp.zeros_like(carry_ref)

    lg = [lbl_ref[l:l + 1, :] for l in range(lbl_ref.shape[0])]
    mx = functools.reduce(jnp.maximum, lg)
    ex = [jnp.exp(v - mx) for v in lg]
    den = functools.reduce(lambda a, b: a + b, ex)
    lb = jnp.zeros_like(mx)
    for l in range(1, layer + 1):
        lb = lb + ex[l] / den
    gh = gh_ref[...]
    nlev = len(LEVELS)
    nblk = CHUNK // DIAG
    tl = lax.broadcasted_iota(jnp.int32, (DIAG, HEAD_DIM), 0)
    st = st_ref[...]

    for c in range(rows_t // CHUNK):
        rows = slice(c * CHUNK, (c + 1) * CHUNK)
        qin = q_ref[rows, :]
        gin = g_ref[rows, :]
        f = lb + (1.0 - lb) * _sigmoid(f_ref[rows, :])
        lf = jnp.log2(f)
        kk = 1.0 - f
        qs = qin * _sigmoid(qin)
        vb = i_ref[rows, :].astype(BF16)

        hi = lf.astype(BF16)
        r1 = lf - hi.astype(F32)
        mid = r1.astype(BF16)
        lo = (r1 - mid.astype(F32)).astype(BF16)
        xs = jnp.dot(gstack_ref[...], jnp.concatenate([hi, mid, lo], axis=0),
                     preferred_element_type=F32)
        b = xs[0:CHUNK]
        bhat = xs[CHUNK:2 * CHUNK]
        bl = xs[2 * CHUNK:3 * CHUNK]

        qe = (qs * jnp.exp2(b)).astype(BF16)
        o = lax.dot_general(qe, st.astype(BF16), NT_DIMS, preferred_element_type=F32)
        khat = (kk * jnp.exp2(bhat)).astype(BF16)
        upd = lax.dot_general(vb, khat, TN_DIMS, preferred_element_type=F32)
        st = st * jnp.exp2(b[CHUNK - 1:CHUNK, :]) + upd

        a = jnp.zeros((CHUNK, CHUNK), F32)
        for li in range(nlev):
            xl = xs[(3 + li) * CHUNK:(4 + li) * CHUNK]
            w = (jnp.where(rsel_ref[li] > 0.5, qs, kk) * jnp.exp2(xl)).astype(BF16)
            a = a + lax.dot_general(w, w, NT_DIMS, preferred_element_type=F32) * lmask_ref[li]

        bk = bl - jnp.log2(jnp.maximum(kk, 0.0))
        pieces = []
        for s in range(DIAG):
            blocks = []
            for blk in range(nblk):
                r = blk * DIAG
                p = jnp.exp2(bl[r:r + DIAG] - bk[r + s:r + s + 1]) * qs[r:r + DIAG]
                blocks.append(jnp.where(tl >= s, p, 0.0))
            pieces.append(jnp.concatenate(blocks, axis=0).astype(BF16))
        ad = jnp.dot(jnp.concatenate(pieces, axis=1), emat_ref[...], preferred_element_type=F32)
        a = a + ad * dmask_ref[...]
        o = o + jnp.dot(a.astype(BF16), vb, preferred_element_type=F32)

        y = o * lax.rsqrt(jnp.mean(o * o, axis=-1, keepdims=True) + EPS) * gh
        oh_ref[rows, :] = (y * (gin * _sigmoid(gin))).astype(oh_ref.dtype)

    st_ref[...] = st

    u = cc_ref[...] * ch_ref[...]
    row = lax.broadcasted_iota(jnp.int32, u.shape, 0)
    prev1 = carry_ref[7:8, :]
    prev2 = carry_ref[6:7, :]
    u1 = jnp.where(row == 0, prev1, pltpu.roll(u, 1, axis=0))
    u2 = jnp.where(row == 0, prev2, jnp.where(row == 1, prev1, pltpu.roll(u, 2, axis=0)))
    carry_ref[...] = u[rows_t - 8:rows_t, :]
    yc = cw_ref[0:1, :] * u2 + cw_ref[1:2, :] * u1 + cw_ref[2:3, :] * u
    z = cb_ref[...] * yc
    zn = z * lax.rsqrt(jnp.mean(z * z, axis=-1, keepdims=True) + EPS) * gc_ref[...]
    oc_ref[...] = zn.astype(oc_ref.dtype)


def _hgrn_conv(proj3, lb_logits, gh, cw, gc, layer, consts, *, tt):
    bsz, seq, _ = proj3.shape
    nh = gh.shape[1] // HEAD_DIM

    def col(k):
        return pl.BlockSpec((None, tt, HEAD_DIM), lambda b, h, t, k=k: (b, t, h + nh * k))

    def par(nrows):
        return pl.BlockSpec((nrows, HEAD_DIM), lambda b, h, t: (0, h))

    out_sd = jax.ShapeDtypeStruct((bsz, seq, nh * HEAD_DIM), BF16)
    out_spec = pl.BlockSpec((None, tt, HEAD_DIM), lambda b, h, t: (b, t, h))
    return pl.pallas_call(
        functools.partial(_hgrn_kernel, layer),
        out_shape=(out_sd, out_sd),
        grid=(bsz, nh, seq // tt),
        in_specs=[col(k) for k in range(7)]
        + [par(lb_logits.shape[0]), par(1), par(CONV_K), par(1)]
        + [_resident(c.shape) for c in consts],
        out_specs=(out_spec, out_spec),
        scratch_shapes=[pltpu.VMEM((HEAD_DIM, HEAD_DIM), F32), pltpu.VMEM((8, HEAD_DIM), F32)],
        compiler_params=_params("arbitrary", "arbitrary", "arbitrary"),
        name="hgrn_conv",
    )(*([proj3] * 7), lb_logits, gh, cw, gc, *consts)


def _mixout_kernel(oh_ref, oc_ref, x_ref, w_ref, g_ref, o_ref):
    kh = oh_ref.shape[1]
    mix = jnp.dot(oh_ref[...], w_ref[0:kh, :], preferred_element_type=F32)
    mix = mix + jnp.dot(oc_ref[...], w_ref[kh:, :], preferred_element_type=F32)
    o_ref[...] = x_ref[...] + _rms(mix, g_ref[...])


def _mixout(oh, oc, x2, w, g, *, tm):
    m, d = x2.shape
    kh = oh.shape[1]
    return pl.pallas_call(
        _mixout_kernel,
        out_shape=jax.ShapeDtypeStruct((m, d), F32),
        grid=(m // tm,),
        in_specs=[pl.BlockSpec((tm, kh), lambda i: (i, 0)),
                  pl.BlockSpec((tm, kh), lambda i: (i, 0)),
                  pl.BlockSpec((tm, d), lambda i: (i, 0)),
                  _resident(w.shape), _resident(g.shape)],
        out_specs=pl.BlockSpec((tm, d), lambda i: (i, 0)),
        compiler_params=_params("arbitrary"),
        name="mixout",
    )(oh, oc, x2, w, g)


def _kv_kernel(m_ref, g_ref, w_ref, o_ref):
    mn = _rms(m_ref[...], g_ref[...]).astype(BF16)
    o_ref[...] = jnp.dot(mn, w_ref[...], preferred_element_type=F32).astype(o_ref.dtype)


def _kvproj(mem2, g, wkv, *, tm):
    m, d = mem2.shape
    return pl.pallas_call(
        _kv_kernel,
        out_shape=jax.ShapeDtypeStruct((2, m, d), BF16),
        grid=(2, m // tm),
        in_specs=[pl.BlockSpec((tm, d), lambda s, i: (i, 0)),
                  pl.BlockSpec((1, d), lambda s, i: (0, 0)),
                  pl.BlockSpec((None, d, d), lambda s, i: (s, 0, 0))],
        out_specs=pl.BlockSpec((None, tm, d), lambda s, i: (s, i, 0)),
        compiler_params=_params("arbitrary", "arbitrary"),
        name="kvproj",
    )(mem2, g, wkv)


def _xattn_kernel(x_ref, gpre_ref, wq_ref, k_ref, v_ref, wo_ref, gpost_ref, o_ref):
    x = x_ref[...]
    d = x.shape[1]
    hd = d // XATTN_HEADS
    q = jnp.dot(_rms(x, gpre_ref[...]).astype(BF16), wq_ref[...], preferred_element_type=F32)
    heads = []
    for h in range(XATTN_HEADS):
        cols = slice(h * hd, (h + 1) * hd)
        s = lax.dot_general(q[:, cols].astype(BF16), k_ref[:, cols], NT_DIMS,
                            preferred_element_type=F32) * (hd ** -0.5)
        p = jnp.exp(s - jnp.max(s, axis=-1, keepdims=True))
        p = p / jnp.sum(p, axis=-1, keepdims=True)
        heads.append(jnp.dot(p.astype(BF16), v_ref[:, cols],
                             preferred_element_type=F32).astype(BF16))
    a = jnp.dot(jnp.concatenate(heads, axis=1), wo_ref[...], preferred_element_type=F32)
    o_ref[...] = x + _rms(a, gpost_ref[...])


def _xattn(x3, gpre, wq, kv, wo, gpost, *, tq):
    bsz, seq, d = x3.shape
    nmem = kv.shape[2]
    return pl.pallas_call(
        _xattn_kernel,
        out_shape=jax.ShapeDtypeStruct((bsz, seq, d), F32),
        grid=(bsz, seq // tq),
        in_specs=[pl.BlockSpec((None, tq, d), lambda b, t: (b, t, 0)),
                  _resident(gpre.shape), _resident(wq.shape),
                  pl.BlockSpec((None, None, nmem, d), lambda b, t: (0, b, 0, 0)),
                  pl.BlockSpec((None, None, nmem, d), lambda b, t: (1, b, 0, 0)),
                  _resident(wo.shape), _resident(gpost.shape)],
        out_specs=pl.BlockSpec((None, tq, d), lambda b, t: (b, t, 0)),
        compiler_params=_params("arbitrary", "arbitrary"),
        name="xattn",
    )(x3, gpre, wq, kv, kv, wo, gpost)


def _mlp_kernel(x_ref, gpre_ref, wu_ref, wd_ref, gpost_ref, o_ref, hn_ref, acc_ref):
    j = pl.program_id(1)

    @pl.when(j == 0)
    def _():
        hn_ref[...] = _rms(x_ref[...], gpre_ref[...]).astype(BF16)

    u = jnp.maximum(jnp.dot(hn_ref[...], wu_ref[...], preferred_element_type=F32), 0.0)
    part = jnp.dot((u * u).astype(BF16), wd_ref[...], preferred_element_type=F32)

    @pl.when(j == 0)
    def _():
        acc_ref[...] = part

    @pl.when(j > 0)
    def _():
        acc_ref[...] += part

    @pl.when(j == pl.num_programs(1) - 1)
    def _():
        o_ref[...] = x_ref[...] + _rms(acc_ref[...], gpost_ref[...])


def _mlp(x2, gpre, wu, wd, gpost, *, tm, tf):
    m, d = x2.shape
    ff = wu.shape[1]
    return pl.pallas_call(
        _mlp_kernel,
        out_shape=jax.ShapeDtypeStruct((m, d), F32),
        grid=(m // tm, ff // tf),
        in_specs=[pl.BlockSpec((tm, d), lambda i, j: (i, 0)),
                  _resident(gpre.shape),
                  pl.BlockSpec((d, tf), lambda i, j: (0, j)),
                  pl.BlockSpec((tf, d), lambda i, j: (j, 0)),
                  _resident(gpost.shape)],
        out_specs=pl.BlockSpec((tm, d), lambda i, j: (i, 0)),
        scratch_shapes=[pltpu.VMEM((tm, d), BF16), pltpu.VMEM((tm, d), F32)],
        compiler_params=_params("arbitrary", "arbitrary"),
        name="mlp",
    )(x2, gpre, wu, wd, gpost)


def _tile(n, want):
    t = min(n, want)
    while n % t:
        t //= 2
    return t


def kernel(x, mem, g_mix_pre, w_in, hgrn_lb_logits, hgrn_norm_g, conv_w, conv_norm_g, w_mix_out,
           g_mix_post, g_x_pre, g_mem, w_q, w_k, w_v, w_xo, g_x_post, g_mlp_pre, w_up, w_down,
           g_mlp_post):
    bsz, seq, d = x.shape
    depth = w_in.shape[0]
    nmem = mem.shape[1]
    m = bsz * seq

    w_in_b = w_in.astype(BF16)
    w_mix_b = w_mix_out.astype(BF16)
    w_q_b = w_q.astype(BF16)
    w_kv_b = jnp.stack([w_k, w_v], axis=1).astype(BF16)
    w_xo_b = w_xo.astype(BF16)
    w_up_b = w_up.astype(BF16)
    w_down_b = w_down.astype(BF16)
    consts = _hgrn_constants()

    def row(p, l):
        return p[l][None, :]

    mem2 = mem.reshape(bsz * nmem, d)
    x2 = x.reshape(m, d)
    for l in range(depth):
        proj = _inproj(x2, row(g_mix_pre, l), w_in_b[l], tm=_tile(m, 1024), tn=_tile(w_in.shape[2], 1024))
        oh, oc = _hgrn_conv(proj.reshape(bsz, seq, -1), hgrn_lb_logits, row(hgrn_norm_g, l),
                            conv_w[l], row(conv_norm_g, l), l, consts, tt=_tile(seq, 512))
        x2 = _mixout(oh.reshape(m, -1), oc.reshape(m, -1), x2, w_mix_b[l], row(g_mix_post, l),
                     tm=_tile(m, 512))
        kv = _kvproj(mem2, row(g_mem, l), w_kv_b[l], tm=_tile(bsz * nmem, 1024))
        x3 = _xattn(x2.reshape(bsz, seq, d), row(g_x_pre, l), w_q_b[l],
                    kv.reshape(2, bsz, nmem, d), w_xo_b[l], row(g_x_post, l), tq=_tile(seq, 512))
        x2 = _mlp(x3.reshape(m, d), row(g_mlp_pre, l), w_up_b[l], w_down_b[l], row(g_mlp_post, l),
                  tm=_tile(m, 512), tf=_tile(w_up.shape[2], 1024))
    return x2.reshape(bsz, seq, d)
```

```python
import functools

import numpy as np
import jax
import jax.numpy as jnp
from jax import lax
from jax.experimental import pallas as pl
from jax.experimental.pallas import tpu as pltpu

F32 = jnp.float32
BF16 = jnp.bfloat16

EPS = 1e-6
HEAD_DIM = 128
CHUNK = 64
DIAG = 8
LEVELS = tuple(h for h in (32, 16, 8) if h >= DIAG)
LOG_TERMS = 2
CONV_K = 3
XATTN_HEADS = 4
V7X_VMEM_LIMIT_BYTES = 60 * 1024 * 1024

NT_DIMS = (((1,), (1,)), ((), ()))
TN_DIMS = (((0,), (0,)), ((), ()))


def _params(*semantics):
    return pltpu.CompilerParams(dimension_semantics=semantics,
                                vmem_limit_bytes=V7X_VMEM_LIMIT_BYTES)


def _rms(x, g):
    return x * lax.rsqrt(jnp.mean(x * x, axis=-1, keepdims=True) + EPS) * g


def _sigmoid(x):
    return 1.0 / (1.0 + jnp.exp(-x))


def _resident(shape):
    return pl.BlockSpec(shape, lambda *_: (0,) * len(shape), pipeline_mode=pl.Buffered(1))


def _inproj_kernel(x_ref, g_ref, w_ref, o_ref, hn_ref):
    @pl.when(pl.program_id(1) == 0)
    def _():
        hn_ref[...] = _rms(x_ref[...], g_ref[...]).astype(BF16)

    o_ref[...] = jnp.dot(hn_ref[...], w_ref[...], preferred_element_type=F32).astype(o_ref.dtype)


def _inproj(x2, g, w, *, tm, tn):
    m, d = x2.shape
    n = w.shape[1]
    return pl.pallas_call(
        _inproj_kernel,
        out_shape=jax.ShapeDtypeStruct((m, n), F32),
        grid=(m // tm, n // tn),
        in_specs=[pl.BlockSpec((tm, d), lambda i, j: (i, 0)),
                  pl.BlockSpec((1, d), lambda i, j: (0, 0)),
                  pl.BlockSpec((d, tn), lambda i, j: (0, j))],
        out_specs=pl.BlockSpec((tm, tn), lambda i, j: (i, j)),
        scratch_shapes=[pltpu.VMEM((tm, d), BF16)],
        compiler_params=_params("arbitrary", "arbitrary"),
        name="inproj",
    )(x2, g, w)


def _hgrn_constants():
    t = np.arange(CHUNK)
    tri = (t[None, :] <= t[:, None])
    gloc = tri & (t[None, :] // DIAG == t[:, None] // DIAG)
    mats = [tri, gloc]
    lmask = []
    for h in LEVELS:
        blk = t // h
        odd = (blk % 2 == 1)
        g = np.zeros((CHUNK, CHUNK), bool)
        for i in range(CHUNK):
            if odd[i]:
                g[i, blk[i] * h:i + 1] = True
            else:
                g[i, i + 1:(blk[i] + 1) * h] = True
        mats.append(g)
        lmask.append(odd[:, None] & (blk[None, :] == blk[:, None] - 1))
    gstack = np.concatenate(mats, axis=0).astype(np.float32)
    gstack = np.concatenate([gstack] * LOG_TERMS, axis=1)
    col = np.arange(CHUNK)
    emat = (np.arange(DIAG * HEAD_DIM)[:, None] // HEAD_DIM == col[None, :] % DIAG)
    dmask = (col[None, :] // DIAG == t[:, None] // DIAG)
    return (jnp.asarray(gstack, BF16), jnp.asarray(np.stack(lmask), F32),
            jnp.asarray(emat, BF16), jnp.asarray(dmask, F32))


def _hgrn_kernel(layer, q_ref, f_ref, i_ref, g_ref, cb_ref, cc_ref, ch_ref,
                 lbl_ref, gh_ref, cw_ref, gc_ref,
                 gstack_ref, lmask_ref, emat_ref, dmask_ref,
                 oh_ref, oc_ref, st_ref, carry_ref):
    rows_t = q_ref.shape[0]

    @pl.when(pl.program_id(2) == 0)
    def _():
        st_ref[...] = jnp.zeros_like(st_ref)
        carry_ref[...] = jnp.zeros_like(carry_ref)

    lg = [lbl_ref[l:l + 1, :] for l in range(lbl_ref.shape[0])]
    mx = functools.reduce(jnp.maximum, lg)
    ex = [jnp.exp(v - mx) for v in lg]
    den = functools.reduce(lambda a, b: a + b, ex)
    lb = jnp.zeros_like(mx)
    for l in range(1, layer + 1):
        lb = lb + ex[l] / den
    gh = gh_ref[...]
    nchunk = rows_t // CHUNK
    nblk = CHUNK // DIAG
    tl = lax.broadcasted_iota(jnp.int32, (DIAG, HEAD_DIM), 0)

    qin = q_ref[...]
    f = lb + (1.0 - lb) * _sigmoid(f_ref[...])
    lf = jnp.log2(f)
    kk_all = 1.0 - f
    lk_all = jnp.log2(jnp.maximum(kk_all, 0.0))
    qs_all = qin * _sigmoid(qin)

    terms, rest = [], lf
    for _ in range(LOG_TERMS):
        terms.append(rest.astype(BF16))
        rest = rest - terms[-1].astype(F32)
    rhs = jnp.concatenate(
        [jnp.concatenate([tm[c * CHUNK:(c + 1) * CHUNK] for tm in terms], axis=0)
         for c in range(nchunk)], axis=1)
    xs_all = jnp.dot(gstack_ref[...], rhs, preferred_element_type=F32)

    qe_c, vb_c, dec_c, upd_c, a_c = [], [], [], [], []
    for c in range(nchunk):
        rows = slice(c * CHUNK, (c + 1) * CHUNK)
        lanes = slice(c * HEAD_DIM, (c + 1) * HEAD_DIM)
        qs = qs_all[rows]
        kk = kk_all[rows]
        vb = i_ref[rows, :].astype(BF16)
        b = xs_all[0:CHUNK, lanes]
        bl = xs_all[CHUNK:2 * CHUNK, lanes]

        blast = b[CHUNK - 1:CHUNK, :]
        qe_c.append((qs * jnp.exp2(b)).astype(BF16))
        khat = (kk * jnp.exp2(blast - b)).astype(BF16)
        upd_c.append(lax.dot_general(vb, khat, TN_DIMS, preferred_element_type=F32))
        dec_c.append(jnp.exp2(blast))
        vb_c.append(vb)

        a = jnp.zeros((CHUNK, CHUNK), F32)
        for li, h in enumerate(LEVELS):
            xl = xs_all[(2 + li) * CHUNK:(3 + li) * CHUNK, lanes]
            base = jnp.concatenate([(qs if (r // h) % 2 else kk)[r:r + h]
                                    for r in range(0, CHUNK, h)], axis=0)
            w = (base * jnp.exp2(xl)).astype(BF16)
            a = a + lax.dot_general(w, w, NT_DIMS, preferred_element_type=F32) * lmask_ref[li]

        bk = bl - lk_all[rows]
        pieces = []
        for s in range(DIAG):
            blocks = []
            for blk in range(nblk):
                r = blk * DIAG
                p = jnp.exp2(bl[r:r + DIAG] - bk[r + s:r + s + 1]) * qs[r:r + DIAG]
                blocks.append(p if s == 0 else jnp.where(tl >= s, p, 0.0))
            pieces.append(jnp.concatenate(blocks, axis=0).astype(BF16))
        ad = jnp.dot(jnp.concatenate(pieces, axis=1), emat_ref[...], preferred_element_type=F32)
        a_c.append((a + ad * dmask_ref[...]).astype(BF16))

    st = st_ref[...]
    o_c = []
    for c in range(nchunk):
        o = lax.dot_general(qe_c[c], st.astype(BF16), NT_DIMS, preferred_element_type=F32)
        o_c.append(o + jnp.dot(a_c[c], vb_c[c], preferred_element_type=F32))
        st = st * dec_c[c] + upd_c[c]
    st_ref[...] = st

    for c in range(nchunk):
        rows = slice(c * CHUNK, (c + 1) * CHUNK)
        o = o_c[c]
        gin = g_ref[rows, :]
        y = o * lax.rsqrt(jnp.mean(o * o, axis=-1, keepdims=True) + EPS) * gh
        oh_ref[rows, :] = (y * (gin * _sigmoid(gin))).astype(oh_ref.dtype)

    u = cc_ref[...] * ch_ref[...]
    row = lax.broadcasted_iota(jnp.int32, u.shape, 0)
    prev1 = carry_ref[7:8, :]
    prev2 = carry_ref[6:7, :]
    u1 = jnp.where(row == 0, prev1, pltpu.roll(u, 1, axis=0))
    u2 = jnp.where(row == 0, prev2, jnp.where(row == 1, prev1, pltpu.roll(u, 2, axis=0)))
    carry_ref[...] = u[rows_t - 8:rows_t, :]
    yc = cw_ref[0:1, :] * u2 + cw_ref[1:2, :] * u1 + cw_ref[2:3, :] * u
    z = cb_ref[...] * yc
    zn = z * lax.rsqrt(jnp.mean(z * z, axis=-1, keepdims=True) + EPS) * gc_ref[...]
    oc_ref[...] = zn.astype(oc_ref.dtype)


def _hgrn_conv(proj3, lb_logits, gh, cw, gc, layer, consts, *, tt):
    bsz, seq, _ = proj3.shape
    nh = gh.shape[1] // HEAD_DIM

    def col(k):
        return pl.BlockSpec((None, tt, HEAD_DIM), lambda b, h, t, k=k: (b, t, h + nh * k))

    def par(nrows):
        return pl.BlockSpec((nrows, HEAD_DIM), lambda b, h, t: (0, h))

    out_sd = jax.ShapeDtypeStruct((bsz, seq, nh * HEAD_DIM), BF16)
    out_spec = pl.BlockSpec((None, tt, HEAD_DIM), lambda b, h, t: (b, t, h))
    return pl.pallas_call(
        functools.partial(_hgrn_kernel, layer),
        out_shape=(out_sd, out_sd),
        grid=(bsz, nh, seq // tt),
        in_specs=[col(k) for k in range(7)]
        + [par(lb_logits.shape[0]), par(1), par(CONV_K), par(1)]
        + [_resident(c.shape) for c in consts],
        out_specs=(out_spec, out_spec),
        scratch_shapes=[pltpu.VMEM((HEAD_DIM, HEAD_DIM), F32), pltpu.VMEM((8, HEAD_DIM), F32)],
        compiler_params=_params("arbitrary", "arbitrary", "arbitrary"),
        name="hgrn_conv",
    )(*([proj3] * 7), lb_logits, gh, cw, gc, *consts)


def _mixout_kernel(oh_ref, oc_ref, x_ref, w_ref, g_ref, o_ref):
    kh = oh_ref.shape[1]
    mix = jnp.dot(oh_ref[...], w_ref[0:kh, :], preferred_element_type=F32)
    mix = mix + jnp.dot(oc_ref[...], w_ref[kh:, :], preferred_element_type=F32)
    o_ref[...] = x_ref[...] + _rms(mix, g_ref[...])


def _mixout(oh, oc, x2, w, g, *, tm):
    m, d = x2.shape
    kh = oh.shape[1]
    return pl.pallas_call(
        _mixout_kernel,
        out_shape=jax.ShapeDtypeStruct((m, d), F32),
        grid=(m // tm,),
        in_specs=[pl.BlockSpec((tm, kh), lambda i: (i, 0)),
                  pl.BlockSpec((tm, kh), lambda i: (i, 0)),
                  pl.BlockSpec((tm, d), lambda i: (i, 0)),
                  _resident(w.shape), _resident(g.shape)],
        out_specs=pl.BlockSpec((tm, d), lambda i: (i, 0)),
        compiler_params=_params("arbitrary"),
        name="mixout",
    )(oh, oc, x2, w, g)


def _kv_kernel(m_ref, g_ref, w_ref, o_ref):
    mn = _rms(m_ref[...], g_ref[...]).astype(BF16)
    o_ref[...] = jnp.dot(mn, w_ref[...], preferred_element_type=F32).astype(o_ref.dtype)


def _kvproj(mem2, g, wkv, *, tm):
    m, d = mem2.shape
    return pl.pallas_call(
        _kv_kernel,
        out_shape=jax.ShapeDtypeStruct((2, m, d), BF16),
        grid=(2, m // tm),
        in_specs=[pl.BlockSpec((tm, d), lambda s, i: (i, 0)),
                  pl.BlockSpec((1, d), lambda s, i: (0, 0)),
                  pl.BlockSpec((None, d, d), lambda s, i: (s, 0, 0))],
        out_specs=pl.BlockSpec((None, tm, d), lambda s, i: (s, i, 0)),
        compiler_params=_params("arbitrary", "arbitrary"),
        name="kvproj",
    )(mem2, g, wkv)


def _xattn_kernel(x_ref, gpre_ref, wq_ref, k_ref, v_ref, wo_ref, gpost_ref, o_ref):
    x = x_ref[...]
    d = x.shape[1]
    hd = d // XATTN_HEADS
    q = jnp.dot(_rms(x, gpre_ref[...]).astype(BF16), wq_ref[...], preferred_element_type=F32)
    heads = []
    for h in range(XATTN_HEADS):
        cols = slice(h * hd, (h + 1) * hd)
        s = lax.dot_general(q[:, cols].astype(BF16), k_ref[:, cols], NT_DIMS,
                            preferred_element_type=F32) * (hd ** -0.5)
        p = jnp.exp(s - jnp.max(s, axis=-1, keepdims=True))
        p = p / jnp.sum(p, axis=-1, keepdims=True)
        heads.append(jnp.dot(p.astype(BF16), v_ref[:, cols],
                             preferred_element_type=F32).astype(BF16))
    a = jnp.dot(jnp.concatenate(heads, axis=1), wo_ref[...], preferred_element_type=F32)
    o_ref[...] = x + _rms(a, gpost_ref[...])


def _xattn(x3, gpre, wq, kv, wo, gpost, *, tq):
    bsz, seq, d = x3.shape
    nmem = kv.shape[2]
    return pl.pallas_call(
        _xattn_kernel,
        out_shape=jax.ShapeDtypeStruct((bsz, seq, d), F32),
        grid=(bsz, seq // tq),
        in_specs=[pl.BlockSpec((None, tq, d), lambda b, t: (b, t, 0)),
                  _resident(gpre.shape), _resident(wq.shape),
                  pl.BlockSpec((None, None, nmem, d), lambda b, t: (0, b, 0, 0)),
                  pl.BlockSpec((None, None, nmem, d), lambda b, t: (1, b, 0, 0)),
                  _resident(wo.shape), _resident(gpost.shape)],
        out_specs=pl.BlockSpec((None, tq, d), lambda b, t: (b, t, 0)),
        compiler_params=_params("arbitrary", "arbitrary"),
        name="xattn",
    )(x3, gpre, wq, kv, kv, wo, gpost)


def _mlp_kernel(x_ref, gpre_ref, wu_ref, wd_ref, gpost_ref, o_ref, hn_ref):
    j = pl.program_id(1)

    @pl.when(j == 0)
    def _():
        hn_ref[...] = _rms(x_ref[...], gpre_ref[...]).astype(BF16)
        o_ref[...] = jnp.zeros_like(o_ref)

    u = jnp.maximum(jnp.dot(hn_ref[...], wu_ref[...], preferred_element_type=F32), 0.0)
    o_ref[...] += jnp.dot((u * u).astype(BF16), wd_ref[...], preferred_element_type=F32)

    @pl.when(j == pl.num_programs(1) - 1)
    def _():
        o_ref[...] = x_ref[...] + _rms(o_ref[...], gpost_ref[...])


def _mlp(x2, gpre, wu, wd, gpost, *, tm, tf):
    m, d = x2.shape
    ff = wu.shape[1]
    return pl.pallas_call(
        _mlp_kernel,
        out_shape=jax.ShapeDtypeStruct((m, d), F32),
        grid=(m // tm, ff // tf),
        in_specs=[pl.BlockSpec((tm, d), lambda i, j: (i, 0)),
                  _resident(gpre.shape),
                  pl.BlockSpec((d, tf), lambda i, j: (0, j)),
                  pl.BlockSpec((tf, d), lambda i, j: (j, 0)),
                  _resident(gpost.shape)],
        out_specs=pl.BlockSpec((tm, d), lambda i, j: (i, 0)),
        scratch_shapes=[pltpu.VMEM((tm, d), BF16)],
        compiler_params=_params("arbitrary", "arbitrary"),
        name="mlp",
    )(x2, gpre, wu, wd, gpost)


def _tile(n, want):
    t = min(n, want)
    while n % t:
        t //= 2
    return t


def _tiles(m, seq, n_mem_rows, in_width, d_ff):
    return dict(
        inproj=dict(tm=_tile(m, 1024), tn=in_width // 4 if in_width % 512 == 0 else _tile(in_width, 1024)),
        hgrn=dict(tt=_tile(seq, 512)),
        mixout=dict(tm=_tile(m, 512)),
        kvproj=dict(tm=_tile(n_mem_rows, 1024)),
        xattn=dict(tq=_tile(seq, 512)),
        mlp=dict(tm=_tile(m, 512), tf=_tile(d_ff, 2048)),
    )


def kernel(x, mem, g_mix_pre, w_in, hgrn_lb_logits, hgrn_norm_g, conv_w, conv_norm_g, w_mix_out,
           g_mix_post, g_x_pre, g_mem, w_q, w_k, w_v, w_xo, g_x_post, g_mlp_pre, w_up, w_down,
           g_mlp_post):
    bsz, seq, d = x.shape
    depth = w_in.shape[0]
    nmem = mem.shape[1]
    m = bsz * seq

    w_in_b = w_in.astype(BF16)
    w_mix_b = w_mix_out.astype(BF16)
    w_q_b = w_q.astype(BF16)
    w_kv_b = jnp.stack([w_k, w_v], axis=1).astype(BF16)
    w_xo_b = w_xo.astype(BF16)
    w_up_b = w_up.astype(BF16)
    w_down_b = w_down.astype(BF16)
    consts = _hgrn_constants()

    def row(p, l):
        return p[l][None, :]

    t = _tiles(m, seq, bsz * nmem, w_in.shape[2], w_up.shape[2])
    mem2 = mem.reshape(bsz * nmem, d)
    x2 = x.reshape(m, d)
    for l in range(depth):
        proj = _inproj(x2, row(g_mix_pre, l), w_in_b[l], **t["inproj"])
        oh, oc = _hgrn_conv(proj.reshape(bsz, seq, -1), hgrn_lb_logits, row(hgrn_norm_g, l),
                            conv_w[l], row(conv_norm_g, l), l, consts, **t["hgrn"])
        x2 = _mixout(oh.reshape(m, -1), oc.reshape(m, -1), x2, w_mix_b[l], row(g_mix_post, l),
                     **t["mixout"])
        kv = _kvproj(mem2, row(g_mem, l), w_kv_b[l], **t["kvproj"])
        x3 = _xattn(x2.reshape(bsz, seq, d), row(g_x_pre, l), w_q_b[l],
                    kv.reshape(2, bsz, nmem, d), w_xo_b[l], row(g_x_post, l), **t["xattn"])
        x2 = _mlp(x3.reshape(m, d), row(g_mlp_pre, l), w_up_b[l], w_down_b[l], row(g_mlp_post, l),
                  **t["mlp"])
    return x2.reshape(bsz, seq, d)
```

```python
import functools

import numpy as np
import jax
import jax.numpy as jnp
from jax import lax
from jax.experimental import pallas as pl
from jax.experimental.pallas import tpu as pltpu

F32 = jnp.float32
BF16 = jnp.bfloat16

EPS = 1e-6
HEAD_DIM = 128
CHUNK = 64
DIAG = 8
LEVELS = tuple(h for h in (32, 16, 8) if h >= DIAG)
LOG_TERMS = 2
CONV_K = 3
XATTN_HEADS = 4
Q, F, I, G, CB, CC, CH = range(7)
N_COMP = 7
V7X_VMEM_LIMIT_BYTES = 60 * 1024 * 1024

NT_DIMS = (((1,), (1,)), ((), ()))
TN_DIMS = (((0,), (0,)), ((), ()))


def _params(*semantics):
    return pltpu.CompilerParams(dimension_semantics=semantics,
                                vmem_limit_bytes=V7X_VMEM_LIMIT_BYTES)


def _rms(x, g):
    return x * lax.rsqrt(jnp.mean(x * x, axis=-1, keepdims=True) + EPS) * g


def _sigmoid(x):
    return 1.0 / (1.0 + jnp.exp(-x))


def _resident(shape):
    return pl.BlockSpec(shape, lambda *_: (0,) * len(shape), pipeline_mode=pl.Buffered(1))


def _layer_resident(shape, layer):
    return pl.BlockSpec((None,) + tuple(shape[1:]), lambda *_: (layer,) + (0,) * (len(shape) - 1),
                        pipeline_mode=pl.Buffered(1))


def _inproj_kernel(x_ref, g_ref, w_ref, o_ref, hn_ref):
    @pl.when(pl.program_id(1) == 0)
    def _():
        hn_ref[...] = _rms(x_ref[...], g_ref[...]).astype(BF16)

    o_ref[...] = jnp.dot(hn_ref[...], w_ref[...], preferred_element_type=F32).astype(o_ref.dtype)


def _inproj(x2, g, w, layer, *, tm, tn):
    m, d = x2.shape
    n = w.shape[2]
    return pl.pallas_call(
        _inproj_kernel,
        out_shape=jax.ShapeDtypeStruct((m, n), F32),
        grid=(m // tm, n // tn),
        in_specs=[pl.BlockSpec((tm, d), lambda i, j: (i, 0)),
                  pl.BlockSpec((1, d), lambda i, j: (0, 0)),
                  pl.BlockSpec((None, d, tn), lambda i, j: (layer, 0, j))],
        out_specs=pl.BlockSpec((tm, tn), lambda i, j: (i, j)),
        scratch_shapes=[pltpu.VMEM((tm, d), BF16)],
        compiler_params=_params("arbitrary", "arbitrary"),
        name="inproj",
    )(x2, g, w)


def _hgrn_constants():
    t = np.arange(CHUNK)
    tri = (t[None, :] <= t[:, None])
    gloc = tri & (t[None, :] // DIAG == t[:, None] // DIAG)
    mats = [tri, gloc]
    lmask = []
    for h in LEVELS:
        blk = t // h
        odd = (blk % 2 == 1)
        g = np.zeros((CHUNK, CHUNK), bool)
        for i in range(CHUNK):
            if odd[i]:
                g[i, blk[i] * h:i + 1] = True
            else:
                g[i, i + 1:(blk[i] + 1) * h] = True
        mats.append(g)
        lmask.append(odd[:, None] & (blk[None, :] == blk[:, None] - 1))
    gstack = np.concatenate(mats, axis=0).astype(np.float32)
    gstack = np.concatenate([gstack] * LOG_TERMS, axis=1)
    col = np.arange(CHUNK)
    emat = (np.arange(DIAG * HEAD_DIM)[:, None] // HEAD_DIM == col[None, :] % DIAG)
    dmask = (col[None, :] // DIAG == t[:, None] // DIAG)
    return (jnp.asarray(gstack, BF16), jnp.asarray(np.stack(lmask), F32),
            jnp.asarray(emat, BF16), jnp.asarray(dmask, F32))


def _hgrn_kernel(layer, depth, p_ref, par_ref, gstack_ref, lmask_ref, emat_ref, dmask_ref,
                 o_ref, st_ref, carry_ref):
    rows_t = p_ref.shape[0]

    def comp(k, rows=slice(None)):
        return p_ref[rows, k * HEAD_DIM:(k + 1) * HEAD_DIM]

    @pl.when(pl.program_id(2) == 0)
    def _():
        st_ref[...] = jnp.zeros_like(st_ref)
        carry_ref[...] = jnp.zeros_like(carry_ref)

    lg = [par_ref[l:l + 1, :] for l in range(depth)]
    gh = par_ref[depth:depth + 1, :]
    cw = [par_ref[depth + 1 + j:depth + 2 + j, :] for j in range(CONV_K)]
    gc = par_ref[depth + 1 + CONV_K:depth + 2 + CONV_K, :]

    mx = functools.reduce(jnp.maximum, lg)
    ex = [jnp.exp(v - mx) for v in lg]
    den = functools.reduce(lambda a, b: a + b, ex)
    lb = jnp.zeros_like(mx)
    for l in range(1, layer + 1):
        lb = lb + ex[l] / den
    nchunk = rows_t // CHUNK
    nblk = CHUNK // DIAG
    tl = lax.broadcasted_iota(jnp.int32, (DIAG, HEAD_DIM), 0)

    qin = comp(Q)
    f = lb + (1.0 - lb) * _sigmoid(comp(F))
    lf = jnp.log2(f)
    kk_all = 1.0 - f
    lk_all = jnp.log2(jnp.maximum(kk_all, 0.0))
    qs_all = qin * _sigmoid(qin)

    terms, rest = [], lf
    for _ in range(LOG_TERMS):
        terms.append(rest.astype(BF16))
        rest = rest - terms[-1].astype(F32)
    rhs = jnp.concatenate(
        [jnp.concatenate([tm[c * CHUNK:(c + 1) * CHUNK] for tm in terms], axis=0)
         for c in range(nchunk)], axis=1)
    xs_all = jnp.dot(gstack_ref[...], rhs, preferred_element_type=F32)

    qe_c, vb_c, dec_c, upd_c, a_c = [], [], [], [], []
    for c in range(nchunk):
        rows = slice(c * CHUNK, (c + 1) * CHUNK)
        lanes = slice(c * HEAD_DIM, (c + 1) * HEAD_DIM)
        qs = qs_all[rows]
        kk = kk_all[rows]
        vb = comp(I, rows).astype(BF16)
        b = xs_all[0:CHUNK, lanes]
        bl = xs_all[CHUNK:2 * CHUNK, lanes]

        blast = b[CHUNK - 1:CHUNK, :]
        qe_c.append((qs * jnp.exp2(b)).astype(BF16))
        khat = (kk * jnp.exp2(blast - b)).astype(BF16)
        upd_c.append(lax.dot_general(vb, khat, TN_DIMS, preferred_element_type=F32))
        dec_c.append(jnp.exp2(blast))
        vb_c.append(vb)

        a = jnp.zeros((CHUNK, CHUNK), F32)
        for li, h in enumerate(LEVELS):
            xl = xs_all[(2 + li) * CHUNK:(3 + li) * CHUNK, lanes]
            base = jnp.concatenate([(qs if (r // h) % 2 else kk)[r:r + h]
                                    for r in range(0, CHUNK, h)], axis=0)
            w = (base * jnp.exp2(xl)).astype(BF16)
            a = a + lax.dot_general(w, w, NT_DIMS, preferred_element_type=F32) * lmask_ref[li]

        bk = bl - lk_all[rows]
        pieces = []
        for s in range(DIAG):
            blocks = []
            for blk in range(nblk):
                r = blk * DIAG
                p = jnp.exp2(bl[r:r + DIAG] - bk[r + s:r + s + 1]) * qs[r:r + DIAG]
                blocks.append(p if s == 0 else jnp.where(tl >= s, p, 0.0))
            pieces.append(jnp.concatenate(blocks, axis=0).astype(BF16))
        ad = jnp.dot(jnp.concatenate(pieces, axis=1), emat_ref[...], preferred_element_type=F32)
        a_c.append((a + ad * dmask_ref[...]).astype(BF16))

    st = st_ref[...]
    o_c = []
    for c in range(nchunk):
        o = lax.dot_general(qe_c[c], st.astype(BF16), NT_DIMS, preferred_element_type=F32)
        o_c.append(o + jnp.dot(a_c[c], vb_c[c], preferred_element_type=F32))
        st = st * dec_c[c] + upd_c[c]
    st_ref[...] = st

    for c in range(nchunk):
        rows = slice(c * CHUNK, (c + 1) * CHUNK)
        o = o_c[c]
        gin = comp(G, rows)
        y = o * lax.rsqrt(jnp.mean(o * o, axis=-1, keepdims=True) + EPS) * gh
        o_ref[rows, 0:HEAD_DIM] = (y * (gin * _sigmoid(gin))).astype(o_ref.dtype)

    u = comp(CC) * comp(CH)
    prev1 = carry_ref[7:8, :]
    prev2 = carry_ref[6:7, :]
    u1 = pltpu.roll(u, 1, axis=0)
    u2 = pltpu.roll(u, 2, axis=0)
    u1 = jnp.concatenate([jnp.where(tl == 0, prev1, u1[0:8]), u1[8:]], axis=0)
    u2 = jnp.concatenate([jnp.where(tl == 0, prev2, jnp.where(tl == 1, prev1, u2[0:8])), u2[8:]],
                         axis=0)
    carry_ref[...] = u[rows_t - 8:rows_t, :]
    z = comp(CB) * (cw[0] * u2 + cw[1] * u1 + cw[2] * u)
    zn = z * lax.rsqrt(jnp.mean(z * z, axis=-1, keepdims=True) + EPS) * gc
    o_ref[:, HEAD_DIM:2 * HEAD_DIM] = zn.astype(o_ref.dtype)


def _hgrn_conv(proj3, par, layer, depth, consts, *, tt):
    bsz, seq, width = proj3.shape
    nh = width // (N_COMP * HEAD_DIM)
    return pl.pallas_call(
        functools.partial(_hgrn_kernel, layer, depth),
        out_shape=jax.ShapeDtypeStruct((bsz, seq, nh * 2 * HEAD_DIM), BF16),
        grid=(bsz, nh, seq // tt),
        in_specs=[pl.BlockSpec((None, tt, N_COMP * HEAD_DIM), lambda b, h, t: (b, t, h)),
                  pl.BlockSpec((par.shape[0], HEAD_DIM), lambda b, h, t: (0, h))]
        + [_resident(c.shape) for c in consts],
        out_specs=pl.BlockSpec((None, tt, 2 * HEAD_DIM), lambda b, h, t: (b, t, h)),
        scratch_shapes=[pltpu.VMEM((HEAD_DIM, HEAD_DIM), F32), pltpu.VMEM((8, HEAD_DIM), F32)],
        compiler_params=_params("arbitrary", "arbitrary", "arbitrary"),
        name="hgrn_conv",
    )(proj3, par, *consts)


def _mixout_kernel(a_ref, x_ref, w_ref, g_ref, o_ref):
    mix = jnp.dot(a_ref[...], w_ref[...], preferred_element_type=F32)
    o_ref[...] = x_ref[...] + _rms(mix, g_ref[...])


def _mixout(a, x2, w, g, layer, *, tm):
    m, d = x2.shape
    return pl.pallas_call(
        _mixout_kernel,
        out_shape=jax.ShapeDtypeStruct((m, d), F32),
        grid=(m // tm,),
        in_specs=[pl.BlockSpec((tm, a.shape[1]), lambda i: (i, 0)),
                  pl.BlockSpec((tm, d), lambda i: (i, 0)),
                  _layer_resident(w.shape, layer), _resident(g.shape)],
        out_specs=pl.BlockSpec((tm, d), lambda i: (i, 0)),
        compiler_params=_params("arbitrary"),
        name="mixout",
    )(a, x2, w, g)


def _kv_kernel(m_ref, g_ref, w_ref, o_ref):
    mn = _rms(m_ref[...], g_ref[...]).astype(BF16)
    o_ref[...] = jnp.dot(mn, w_ref[...], preferred_element_type=F32).astype(o_ref.dtype)


def _kvproj(mem2, g, wkv, layer, *, tm):
    m, d = mem2.shape
    return pl.pallas_call(
        _kv_kernel,
        out_shape=jax.ShapeDtypeStruct((2, m, d), BF16),
        grid=(2, m // tm),
        in_specs=[pl.BlockSpec((tm, d), lambda s, i: (i, 0)),
                  pl.BlockSpec((1, d), lambda s, i: (0, 0)),
                  pl.BlockSpec((None, None, d, d), lambda s, i: (layer, s, 0, 0))],
        out_specs=pl.BlockSpec((None, tm, d), lambda s, i: (s, i, 0)),
        compiler_params=_params("arbitrary", "arbitrary"),
        name="kvproj",
    )(mem2, g, wkv)


def _xattn_kernel(x_ref, gpre_ref, wq_ref, k_ref, v_ref, wo_ref, gpost_ref, o_ref):
    x = x_ref[...]
    d = x.shape[1]
    hd = d // XATTN_HEADS
    q = jnp.dot(_rms(x, gpre_ref[...]).astype(BF16), wq_ref[...], preferred_element_type=F32)
    heads = []
    for h in range(XATTN_HEADS):
        cols = slice(h * hd, (h + 1) * hd)
        s = lax.dot_general(q[:, cols].astype(BF16), k_ref[:, cols], NT_DIMS,
                            preferred_element_type=F32) * (hd ** -0.5)
        p = jnp.exp(s - jnp.max(s, axis=-1, keepdims=True))
        p = p / jnp.sum(p, axis=-1, keepdims=True)
        heads.append(jnp.dot(p.astype(BF16), v_ref[:, cols],
                             preferred_element_type=F32).astype(BF16))
    a = jnp.dot(jnp.concatenate(heads, axis=1), wo_ref[...], preferred_element_type=F32)
    o_ref[...] = x + _rms(a, gpost_ref[...])


def _xattn(x3, gpre, wq, kv, wo, gpost, layer, *, tq):
    bsz, seq, d = x3.shape
    nmem = kv.shape[2]
    return pl.pallas_call(
        _xattn_kernel,
        out_shape=jax.ShapeDtypeStruct((bsz, seq, d), F32),
        grid=(bsz, seq // tq),
        in_specs=[pl.BlockSpec((None, tq, d), lambda b, t: (b, t, 0)),
                  _resident(gpre.shape), _layer_resident(wq.shape, layer),
                  pl.BlockSpec((None, None, nmem, d), lambda b, t: (0, b, 0, 0)),
                  pl.BlockSpec((None, None, nmem, d), lambda b, t: (1, b, 0, 0)),
                  _layer_resident(wo.shape, layer), _resident(gpost.shape)],
        out_specs=pl.BlockSpec((None, tq, d), lambda b, t: (b, t, 0)),
        compiler_params=_params("arbitrary", "arbitrary"),
        name="xattn",
    )(x3, gpre, wq, kv, kv, wo, gpost)


def _mlp_kernel(x_ref, gpre_ref, wu_ref, wd_ref, gpost_ref, o_ref, hn_ref):
    j = pl.program_id(1)

    @pl.when(j == 0)
    def _():
        hn_ref[...] = _rms(x_ref[...], gpre_ref[...]).astype(BF16)
        o_ref[...] = jnp.zeros_like(o_ref)

    u = jnp.maximum(jnp.dot(hn_ref[...], wu_ref[...], preferred_element_type=F32), 0.0)
    o_ref[...] += jnp.dot((u * u).astype(BF16), wd_ref[...], preferred_element_type=F32)

    @pl.when(j == pl.num_programs(1) - 1)
    def _():
        o_ref[...] = x_ref[...] + _rms(o_ref[...], gpost_ref[...])


def _mlp(x2, gpre, wu, wd, gpost, layer, *, tm, tf):
    m, d = x2.shape
    ff = wu.shape[2]
    return pl.pallas_call(
        _mlp_kernel,
        out_shape=jax.ShapeDtypeStruct((m, d), F32),
        grid=(m // tm, ff // tf),
        in_specs=[pl.BlockSpec((tm, d), lambda i, j: (i, 0)),
                  _resident(gpre.shape),
                  pl.BlockSpec((None, d, tf), lambda i, j: (layer, 0, j)),
                  pl.BlockSpec((None, tf, d), lambda i, j: (layer, j, 0)),
                  _resident(gpost.shape)],
        out_specs=pl.BlockSpec((tm, d), lambda i, j: (i, 0)),
        scratch_shapes=[pltpu.VMEM((tm, d), BF16)],
        compiler_params=_params("arbitrary", "arbitrary"),
        name="mlp",
    )(x2, gpre, wu, wd, gpost)


def _tile(n, want):
    t = min(n, want)
    while n % t:
        t //= 2
    return t


def _tiles(m, seq, n_mem_rows, n_heads, d_ff):
    return dict(
        inproj=dict(tm=_tile(m, 1024), tn=N_COMP * HEAD_DIM * _tile(n_heads, 2)),
        hgrn=dict(tt=_tile(seq, 1024)),
        mixout=dict(tm=_tile(m, 512)),
        kvproj=dict(tm=_tile(n_mem_rows, 1024)),
        xattn=dict(tq=_tile(seq, 512)),
        mlp=dict(tm=_tile(m, 512), tf=_tile(d_ff, 2048)),
    )


def kernel(x, mem, g_mix_pre, w_in, hgrn_lb_logits, hgrn_norm_g, conv_w, conv_norm_g, w_mix_out,
           g_mix_post, g_x_pre, g_mem, w_q, w_k, w_v, w_xo, g_x_post, g_mlp_pre, w_up, w_down,
           g_mlp_post):
    bsz, seq, d = x.shape
    depth = w_in.shape[0]
    nmem = mem.shape[1]
    m = bsz * seq
    nh = hgrn_norm_g.shape[1] // HEAD_DIM
    assert w_in.shape[2] == N_COMP * nh * HEAD_DIM and conv_norm_g.shape[1] == nh * HEAD_DIM

    w_in_b = (w_in.reshape(depth, d, N_COMP, nh, HEAD_DIM).transpose(0, 1, 3, 2, 4)
              .reshape(depth, d, N_COMP * nh * HEAD_DIM).astype(BF16))
    w_mix_b = (w_mix_out.reshape(depth, 2, nh, HEAD_DIM, d).transpose(0, 2, 1, 3, 4)
               .reshape(depth, 2 * nh * HEAD_DIM, d).astype(BF16))
    w_q_b = w_q.astype(BF16)
    w_kv_b = jnp.stack([w_k, w_v], axis=1).astype(BF16)
    w_xo_b = w_xo.astype(BF16)
    w_up_b = w_up.astype(BF16)
    w_down_b = w_down.astype(BF16)
    consts = _hgrn_constants()

    def row(p, l):
        return p[l][None, :]

    t = _tiles(m, seq, bsz * nmem, nh, w_up.shape[2])
    mem2 = mem.reshape(bsz * nmem, d)
    x2 = x.reshape(m, d)
    for l in range(depth):
        proj = _inproj(x2, row(g_mix_pre, l), w_in_b, l, **t["inproj"])
        par = jnp.concatenate([hgrn_lb_logits, row(hgrn_norm_g, l), conv_w[l], row(conv_norm_g, l)],
                              axis=0)
        mixed = _hgrn_conv(proj.reshape(bsz, seq, -1), par, l, depth, consts, **t["hgrn"])
        x2 = _mixout(mixed.reshape(m, -1), x2, w_mix_b, row(g_mix_post, l), l, **t["mixout"])
        kv = _kvproj(mem2, row(g_mem, l), w_kv_b, l, **t["kvproj"])
        x3 = _xattn(x2.reshape(bsz, seq, d), row(g_x_pre, l), w_q_b,
                    kv.reshape(2, bsz, nmem, d), w_xo_b, row(g_x_post, l), l, **t["xattn"])
        x2 = _mlp(x3.reshape(m, d), row(g_mlp_pre, l), w_up_b, w_down_b, row(g_mlp_post, l), l,
                  **t["mlp"])
    return x2.reshape(bsz, seq, d)
```

```python
import functools

import numpy as np
import jax
import jax.numpy as jnp
from jax import lax
from jax.experimental import pallas as pl
from jax.experimental.pallas import tpu as pltpu

F32 = jnp.float32
BF16 = jnp.bfloat16

EPS = 1e-6
HEAD_DIM = 128
CHUNK = 64
DIAG = 8
LEVELS = tuple(h for h in (32, 16, 8) if h >= DIAG)
LOG_TERMS = 2
CONV_K = 3
XATTN_HEADS = 4
Q, F, I, G, CB, CC, CH = range(7)
N_COMP = 7
MXU_COLS = 256
V7X_VMEM_LIMIT_BYTES = 60 * 1024 * 1024

NT_DIMS = (((1,), (1,)), ((), ()))
TN_DIMS = (((0,), (0,)), ((), ()))


def _params(*semantics):
    return pltpu.CompilerParams(dimension_semantics=semantics,
                                vmem_limit_bytes=V7X_VMEM_LIMIT_BYTES)


def _rms(x, g):
    return x * lax.rsqrt(jnp.mean(x * x, axis=-1, keepdims=True) + EPS) * g


def _sigmoid(x):
    return 1.0 / (1.0 + jnp.exp(-x))


def _resident(shape):
    return pl.BlockSpec(shape, lambda *_: (0,) * len(shape), pipeline_mode=pl.Buffered(1))


def _layer_resident(shape, layer):
    return pl.BlockSpec((None,) + tuple(shape[1:]), lambda *_: (layer,) + (0,) * (len(shape) - 1),
                        pipeline_mode=pl.Buffered(1))


def _hgrn_constants():
    t = np.arange(CHUNK)
    tri = (t[None, :] <= t[:, None])
    gloc = tri & (t[None, :] // DIAG == t[:, None] // DIAG)
    mats = [tri, gloc]
    lmask = []
    for h in LEVELS:
        blk = t // h
        odd = (blk % 2 == 1)
        g = np.zeros((CHUNK, CHUNK), bool)
        for i in range(CHUNK):
            if odd[i]:
                g[i, blk[i] * h:i + 1] = True
            else:
                g[i, i + 1:(blk[i] + 1) * h] = True
        mats.append(g)
        lmask.append(odd[:, None] & (blk[None, :] == blk[:, None] - 1))
    gstack = np.concatenate(mats, axis=0).astype(np.float32)
    gstack = np.concatenate([gstack] * LOG_TERMS, axis=1)
    return jnp.asarray(gstack, BF16), jnp.asarray(np.stack(lmask), F32)


def _head_params(par, layer, depth):
    lg = [par[l:l + 1, :] for l in range(depth)]
    mx = functools.reduce(jnp.maximum, lg)
    ex = [jnp.exp(v - mx) for v in lg]
    den = functools.reduce(lambda a, b: a + b, ex)
    lb = jnp.zeros_like(mx)
    for l in range(1, layer + 1):
        lb = lb + ex[l] / den
    gh = par[depth:depth + 1, :]
    cw = [par[depth + 1 + j:depth + 2 + j, :] for j in range(CONV_K)]
    gc = par[depth + 1 + CONV_K:depth + 2 + CONV_K, :]
    return lb, gh, cw, gc


def _hgrn_prepare(comp, lb, gstack_ref, nchunk):
    qin = comp(Q)
    f = lb + (1.0 - lb) * _sigmoid(comp(F))
    lf = jnp.log2(f)
    kk_all = 1.0 - f
    lk_all = jnp.log2(jnp.maximum(kk_all, 0.0))
    qs_all = qin * _sigmoid(qin)
    terms, rest = [], lf
    for _ in range(LOG_TERMS):
        terms.append(rest.astype(BF16))
        rest = rest - terms[-1].astype(F32)
    rhs = jnp.concatenate(
        [jnp.concatenate([tm[c * CHUNK:(c + 1) * CHUNK] for tm in terms], axis=0)
         for c in range(nchunk)], axis=1)
    xs_all = jnp.dot(gstack_ref[...], rhs, preferred_element_type=F32)
    return qs_all, kk_all, lk_all, xs_all


def _hgrn_scores(prep, comp, lmask_ref, nchunk):
    qs_all, kk_all, lk_all, xs_all = prep
    nblk = CHUNK // DIAG
    tl = lax.broadcasted_iota(jnp.int32, (DIAG, HEAD_DIM), 0)
    row = lax.broadcasted_iota(jnp.int32, (CHUNK, CHUNK), 0)
    dcol = lax.broadcasted_iota(jnp.int32, (CHUNK, CHUNK), 1) - (row - lax.rem(row, DIAG))
    qe_c, vb_c, dec_c, upd_c, a_c = [], [], [], [], []
    for c in range(nchunk):
        rows = slice(c * CHUNK, (c + 1) * CHUNK)
        lanes = slice(c * HEAD_DIM, (c + 1) * HEAD_DIM)
        qs = qs_all[rows]
        kk = kk_all[rows]
        vb = comp(I, rows).astype(BF16)
        b = xs_all[0:CHUNK, lanes]
        bl = xs_all[CHUNK:2 * CHUNK, lanes]

        blast = b[CHUNK - 1:CHUNK, :]
        qe_c.append((qs * jnp.exp2(b)).astype(BF16))
        khat = (kk * jnp.exp2(blast - b)).astype(BF16)
        upd_c.append(lax.dot_general(vb, khat, TN_DIMS, preferred_element_type=F32))
        dec_c.append(jnp.exp2(blast))
        vb_c.append(vb)

        a = jnp.zeros((CHUNK, CHUNK), F32)
        for li, h in enumerate(LEVELS):
            xl = xs_all[(2 + li) * CHUNK:(3 + li) * CHUNK, lanes]
            base = jnp.concatenate([(qs if (r // h) % 2 else kk)[r:r + h]
                                    for r in range(0, CHUNK, h)], axis=0)
            w = (base * jnp.exp2(xl)).astype(BF16)
            a = a + lax.dot_general(w, w, NT_DIMS, preferred_element_type=F32) * lmask_ref[li]

        bk = bl - lk_all[rows]
        for s in range(DIAG):
            sums = []
            for blk in range(nblk):
                r = blk * DIAG
                p = jnp.exp2(bl[r:r + DIAG] - bk[r + s:r + s + 1]) * qs[r:r + DIAG]
                p = p if s == 0 else jnp.where(tl >= s, p, 0.0)
                sums.append(jnp.sum(p, axis=-1, keepdims=True))
            a = jnp.where(dcol == s, jnp.concatenate(sums, axis=0), a)
        a_c.append(a.astype(BF16))
    return qe_c, vb_c, dec_c, upd_c, a_c


def _hgrn_outputs(scores, st):
    qe_c, vb_c, dec_c, upd_c, a_c = scores
    o_c = []
    for c in range(len(qe_c)):
        o = lax.dot_general(qe_c[c], st.astype(BF16), NT_DIMS, preferred_element_type=F32)
        o_c.append(o + jnp.dot(a_c[c], vb_c[c], preferred_element_type=F32))
        st = st * dec_c[c] + upd_c[c]
    return o_c, st


def _hgrn_finish(o_c, comp, gh, store):
    for c, o in enumerate(o_c):
        rows = slice(c * CHUNK, (c + 1) * CHUNK)
        gin = comp(G, rows)
        y = o * lax.rsqrt(jnp.mean(o * o, axis=-1, keepdims=True) + EPS) * gh
        store(rows, y * (gin * _sigmoid(gin)))


def _short_conv(comp, cw, gc, carry, store):
    u = comp(CC) * comp(CH)
    rows_t = u.shape[0]
    tl = lax.broadcasted_iota(jnp.int32, (8, HEAD_DIM), 0)
    prev1 = carry[7:8, :]
    prev2 = carry[6:7, :]
    u1 = pltpu.roll(u, 1, axis=0)
    u2 = pltpu.roll(u, 2, axis=0)
    u1 = jnp.concatenate([jnp.where(tl == 0, prev1, u1[0:8]), u1[8:]], axis=0)
    u2 = jnp.concatenate([jnp.where(tl == 0, prev2, jnp.where(tl == 1, prev1, u2[0:8])), u2[8:]],
                         axis=0)
    z = comp(CB) * (cw[0] * u2 + cw[1] * u1 + cw[2] * u)
    store(z * lax.rsqrt(jnp.mean(z * z, axis=-1, keepdims=True) + EPS) * gc)
    return u[rows_t - 8:rows_t, :]


def _mixer_kernel(layer, depth, hp, n_j, n_r, x_ref, g_ref, w_ref, par_ref,
                  gstack_ref, lmask_ref,
                  o_ref, hn_ref, pbuf_a, pbuf_b, st_ref, carry_ref):
    n = pl.program_id(0)
    p = jnp.maximum(n - 1, 0)
    head0 = lax.rem(p, n_j) * hp
    rows_t = hn_ref.shape[0]
    nchunk = rows_t // CHUNK
    tn = w_ref.shape[1]
    n_pieces = -(-tn // MXU_COLS)

    @pl.when(n == 0)
    def _():
        pbuf_b[...] = jnp.zeros_like(pbuf_b)

    @pl.when(lax.rem(n, n_j) == 0)
    def _():
        hn_ref[...] = _rms(x_ref[...], g_ref[...]).astype(BF16)

    @pl.when(lax.rem(p // n_j, n_r) == 0)
    def _():
        for hl in range(hp):
            st_ref[head0 + hl] = jnp.zeros((HEAD_DIM, HEAD_DIM), F32)
            carry_ref[head0 + hl] = jnp.zeros((8, HEAD_DIM), F32)

    def step(src_ref, dst_ref):
        pieces = iter(range(n_pieces))

        def project_some(count):
            for piece in (next(pieces, None) for _ in range(count)):
                if piece is not None:
                    cols = slice(piece * MXU_COLS, min((piece + 1) * MXU_COLS, tn))
                    dst_ref[:, cols] = jnp.dot(hn_ref[...], w_ref[:, cols],
                                               preferred_element_type=F32)

        def comp_of(hl):
            def comp(k, rows=slice(None)):
                c0 = (hl * N_COMP + k) * HEAD_DIM
                return src_ref[rows, c0:c0 + HEAD_DIM]
            return comp

        def store_of(hl, part):
            c0 = (2 * hl + part) * HEAD_DIM

            def store(*args):
                rows, val = args if len(args) == 2 else (slice(None), args[0])
                o_ref[rows, c0:c0 + HEAD_DIM] = val.astype(o_ref.dtype)
            return store

        hp_par = [_head_params(par_ref[head0 + hl], layer, depth) for hl in range(hp)]
        prep = [_hgrn_prepare(comp_of(hl), hp_par[hl][0], gstack_ref, nchunk) for hl in range(hp)]
        per_head = max(1, (n_pieces - 2) // (2 * hp))
        outs = []
        for hl in range(hp):
            project_some(per_head)
            scores = _hgrn_scores(prep[hl], comp_of(hl), lmask_ref, nchunk)
            project_some(per_head)
            o_c, st = _hgrn_outputs(scores, st_ref[head0 + hl])
            st_ref[head0 + hl] = st
            outs.append(o_c)
        project_some(n_pieces)
        for hl in range(hp):
            _hgrn_finish(outs[hl], comp_of(hl), hp_par[hl][1], store_of(hl, 0))
            carry_ref[head0 + hl] = _short_conv(comp_of(hl), hp_par[hl][2], hp_par[hl][3],
                                                carry_ref[head0 + hl], store_of(hl, 1))

    @pl.when(lax.rem(n, 2) == 0)
    def _():
        step(pbuf_b, pbuf_a)

    @pl.when(lax.rem(n, 2) == 1)
    def _():
        step(pbuf_a, pbuf_b)


def _mixer(x2, g, w, par, layer, depth, consts, *, seq, tm, hp):
    m, d = x2.shape
    nh = par.shape[0]
    n_j = nh // hp
    n_r = seq // tm
    tn = hp * N_COMP * HEAD_DIM
    steps = (m // tm) * n_j

    def consumed(n):
        return jnp.maximum(n - 1, 0)

    def produced(n):
        return jnp.minimum(n, steps - 1)

    return pl.pallas_call(
        functools.partial(_mixer_kernel, layer, depth, hp, n_j, n_r),
        out_shape=jax.ShapeDtypeStruct((m, nh * 2 * HEAD_DIM), BF16),
        grid=(steps + 1,),
        in_specs=[pl.BlockSpec((tm, d), lambda n: (produced(n) // n_j, 0)),
                  _resident(g.shape),
                  pl.BlockSpec((None, d, tn), lambda n: (layer, 0, lax.rem(produced(n), n_j))),
                  _resident(par.shape)]
        + [_resident(c.shape) for c in consts],
        out_specs=pl.BlockSpec((tm, hp * 2 * HEAD_DIM),
                               lambda n: (consumed(n) // n_j, lax.rem(consumed(n), n_j))),
        scratch_shapes=[pltpu.VMEM((tm, d), BF16), pltpu.VMEM((tm, tn), F32), pltpu.VMEM((tm, tn), F32),
                        pltpu.VMEM((nh, HEAD_DIM, HEAD_DIM), F32), pltpu.VMEM((nh, 8, HEAD_DIM), F32)],
        compiler_params=_params("arbitrary"),
        name="mixer",
    )(x2, g, w, par, *consts)


def _mixout_kernel(a_ref, x_ref, w_ref, g_ref, o_ref):
    mix = jnp.dot(a_ref[...], w_ref[...], preferred_element_type=F32)
    o_ref[...] = x_ref[...] + _rms(mix, g_ref[...])


def _mixout(a, x2, w, g, layer, *, tm):
    m, d = x2.shape
    return pl.pallas_call(
        _mixout_kernel,
        out_shape=jax.ShapeDtypeStruct((m, d), F32),
        grid=(m // tm,),
        in_specs=[pl.BlockSpec((tm, a.shape[1]), lambda i: (i, 0)),
                  pl.BlockSpec((tm, d), lambda i: (i, 0)),
                  _layer_resident(w.shape, layer), _resident(g.shape)],
        out_specs=pl.BlockSpec((tm, d), lambda i: (i, 0)),
        compiler_params=_params("arbitrary"),
        name="mixout",
    )(a, x2, w, g)


def _kv_kernel(m_ref, g_ref, w_ref, o_ref):
    mn = _rms(m_ref[...], g_ref[...]).astype(BF16)
    o_ref[...] = jnp.dot(mn, w_ref[...], preferred_element_type=F32).astype(o_ref.dtype)


def _kvproj(mem2, g, wkv, layer, *, tm):
    m, d = mem2.shape
    return pl.pallas_call(
        _kv_kernel,
        out_shape=jax.ShapeDtypeStruct((2, m, d), BF16),
        grid=(2, m // tm),
        in_specs=[pl.BlockSpec((tm, d), lambda s, i: (i, 0)),
                  pl.BlockSpec((1, d), lambda s, i: (0, 0)),
                  pl.BlockSpec((None, None, d, d), lambda s, i: (layer, s, 0, 0))],
        out_specs=pl.BlockSpec((None, tm, d), lambda s, i: (s, i, 0)),
        compiler_params=_params("arbitrary", "arbitrary"),
        name="kvproj",
    )(mem2, g, wkv)


def _xattn_kernel(x_ref, gpre_ref, wq_ref, k_ref, v_ref, wo_ref, gpost_ref, o_ref):
    x = x_ref[...]
    d = x.shape[1]
    hd = d // XATTN_HEADS
    q = jnp.dot(_rms(x, gpre_ref[...]).astype(BF16), wq_ref[...], preferred_element_type=F32)
    heads = []
    for h in range(XATTN_HEADS):
        cols = slice(h * hd, (h + 1) * hd)
        s = lax.dot_general(q[:, cols].astype(BF16), k_ref[:, cols], NT_DIMS,
                            preferred_element_type=F32) * (hd ** -0.5)
        p = jnp.exp(s - jnp.max(s, axis=-1, keepdims=True))
        p = p / jnp.sum(p, axis=-1, keepdims=True)
        heads.append(jnp.dot(p.astype(BF16), v_ref[:, cols],
                             preferred_element_type=F32).astype(BF16))
    a = jnp.dot(jnp.concatenate(heads, axis=1), wo_ref[...], preferred_element_type=F32)
    o_ref[...] = x + _rms(a, gpost_ref[...])


def _xattn(x3, gpre, wq, kv, wo, gpost, layer, *, tq):
    bsz, seq, d = x3.shape
    nmem = kv.shape[2]
    return pl.pallas_call(
        _xattn_kernel,
        out_shape=jax.ShapeDtypeStruct((bsz, seq, d), F32),
        grid=(bsz, seq // tq),
        in_specs=[pl.BlockSpec((None, tq, d), lambda b, t: (b, t, 0)),
                  _resident(gpre.shape), _layer_resident(wq.shape, layer),
                  pl.BlockSpec((None, None, nmem, d), lambda b, t: (0, b, 0, 0)),
                  pl.BlockSpec((None, None, nmem, d), lambda b, t: (1, b, 0, 0)),
                  _layer_resident(wo.shape, layer), _resident(gpost.shape)],
        out_specs=pl.BlockSpec((None, tq, d), lambda b, t: (b, t, 0)),
        compiler_params=_params("arbitrary", "arbitrary"),
        name="xattn",
    )(x3, gpre, wq, kv, kv, wo, gpost)


def _mlp_kernel(x_ref, gpre_ref, wu_ref, wd_ref, gpost_ref, o_ref, hn_ref):
    j = pl.program_id(1)

    @pl.when(j == 0)
    def _():
        hn_ref[...] = _rms(x_ref[...], gpre_ref[...]).astype(BF16)
        o_ref[...] = jnp.zeros_like(o_ref)

    u = jnp.maximum(jnp.dot(hn_ref[...], wu_ref[...], preferred_element_type=F32), 0.0)
    o_ref[...] += jnp.dot((u * u).astype(BF16), wd_ref[...], preferred_element_type=F32)

    @pl.when(j == pl.num_programs(1) - 1)
    def _():
        o_ref[...] = x_ref[...] + _rms(o_ref[...], gpost_ref[...])


def _mlp(x2, gpre, wu, wd, gpost, layer, *, tm, tf):
    m, d = x2.shape
    ff = wu.shape[2]
    return pl.pallas_call(
        _mlp_kernel,
        out_shape=jax.ShapeDtypeStruct((m, d), F32),
        grid=(m // tm, ff // tf),
        in_specs=[pl.BlockSpec((tm, d), lambda i, j: (i, 0)),
                  _resident(gpre.shape),
                  pl.BlockSpec((None, d, tf), lambda i, j: (layer, 0, j)),
                  pl.BlockSpec((None, tf, d), lambda i, j: (layer, j, 0)),
                  _resident(gpost.shape)],
        out_specs=pl.BlockSpec((tm, d), lambda i, j: (i, 0)),
        scratch_shapes=[pltpu.VMEM((tm, d), BF16)],
        compiler_params=_params("arbitrary", "arbitrary"),
        name="mlp",
    )(x2, gpre, wu, wd, gpost)


def _tile(n, want):
    t = min(n, want)
    while n % t:
        t //= 2
    return t


def _tiles(m, seq, n_mem_rows, n_heads, d_ff):
    return dict(
        mixer=dict(tm=_tile(seq, 512), hp=_tile(n_heads, 2)),
        mixout=dict(tm=_tile(m, 512)),
        kvproj=dict(tm=_tile(n_mem_rows, 1024)),
        xattn=dict(tq=_tile(seq, 512)),
        mlp=dict(tm=_tile(m, 512), tf=_tile(d_ff, 2048)),
    )


def kernel(x, mem, g_mix_pre, w_in, hgrn_lb_logits, hgrn_norm_g, conv_w, conv_norm_g, w_mix_out,
           g_mix_post, g_x_pre, g_mem, w_q, w_k, w_v, w_xo, g_x_post, g_mlp_pre, w_up, w_down,
           g_mlp_post):
    bsz, seq, d = x.shape
    depth = w_in.shape[0]
    nmem = mem.shape[1]
    m = bsz * seq
    nh = hgrn_norm_g.shape[1] // HEAD_DIM
    assert w_in.shape[2] == N_COMP * nh * HEAD_DIM and conv_norm_g.shape[1] == nh * HEAD_DIM

    w_in_b = (w_in.astype(BF16).reshape(depth, d, N_COMP, nh, HEAD_DIM).transpose(0, 1, 3, 2, 4)
              .reshape(depth, d, N_COMP * nh * HEAD_DIM))
    w_mix_b = (w_mix_out.astype(BF16).reshape(depth, 2, nh, HEAD_DIM, d).transpose(0, 2, 1, 3, 4)
               .reshape(depth, 2 * nh * HEAD_DIM, d))
    w_q_b = w_q.astype(BF16)
    w_kv_b = jnp.stack([w_k, w_v], axis=1).astype(BF16)
    w_xo_b = w_xo.astype(BF16)
    w_up_b = w_up.astype(BF16)
    w_down_b = w_down.astype(BF16)
    consts = _hgrn_constants()

    def row(p, l):
        return p[l][None, :]

    t = _tiles(m, seq, bsz * nmem, nh, w_up.shape[2])
    mem2 = mem.reshape(bsz * nmem, d)
    x2 = x.reshape(m, d)
    for l in range(depth):
        par = jnp.concatenate([hgrn_lb_logits, row(hgrn_norm_g, l), conv_w[l], row(conv_norm_g, l)],
                              axis=0)
        par = par.reshape(par.shape[0], nh, HEAD_DIM).transpose(1, 0, 2)
        mixed = _mixer(x2, row(g_mix_pre, l), w_in_b, par, l, depth, consts, seq=seq, **t["mixer"])
        x2 = _mixout(mixed, x2, w_mix_b, row(g_mix_post, l), l, **t["mixout"])
        kv = _kvproj(mem2, row(g_mem, l), w_kv_b, l, **t["kvproj"])
        x3 = _xattn(x2.reshape(bsz, seq, d), row(g_x_pre, l), w_q_b,
                    kv.reshape(2, bsz, nmem, d), w_xo_b, row(g_x_post, l), l, **t["xattn"])
        x2 = _mlp(x3.reshape(m, d), row(g_mlp_pre, l), w_up_b, w_down_b, row(g_mlp_post, l), l,
                  **t["mlp"])
    return x2.reshape(bsz, seq, d)
```

```python
import functools

import numpy as np
import jax
import jax.numpy as jnp
from jax import lax
from jax.experimental import pallas as pl
from jax.experimental.pallas import tpu as pltpu

F32 = jnp.float32
BF16 = jnp.bfloat16

EPS = 1e-6
HEAD_DIM = 128
CHUNK = 64
DIAG = 8
LEVELS = tuple(h for h in (32, 16, 8) if h >= DIAG)
LOG_TERMS = 2
CONV_K = 3
XATTN_HEADS = 4
Q, F, I, G, CB, CC, CH = range(7)
N_COMP = 7
V7X_VMEM_LIMIT_BYTES = 60 * 1024 * 1024

NT_DIMS = (((1,), (1,)), ((), ()))
TN_DIMS = (((0,), (0,)), ((), ()))


def _params(*semantics):
    return pltpu.CompilerParams(dimension_semantics=semantics,
                                vmem_limit_bytes=V7X_VMEM_LIMIT_BYTES)


def _rms(x, g):
    return x * lax.rsqrt(jnp.mean(x * x, axis=-1, keepdims=True) + EPS) * g


def _sigmoid(x):
    return 1.0 / (1.0 + jnp.exp(-x))


def _resident(shape):
    return pl.BlockSpec(shape, lambda *_: (0,) * len(shape), pipeline_mode=pl.Buffered(1))


def _layer_resident(shape, layer):
    return pl.BlockSpec((None,) + tuple(shape[1:]), lambda *_: (layer,) + (0,) * (len(shape) - 1),
                        pipeline_mode=pl.Buffered(1))


def _hgrn_constants():
    t = np.arange(CHUNK)
    tri = (t[None, :] <= t[:, None])
    gloc = tri & (t[None, :] // DIAG == t[:, None] // DIAG)
    mats = [tri, gloc]
    lmask = []
    for h in LEVELS:
        blk = t // h
        odd = (blk % 2 == 1)
        g = np.zeros((CHUNK, CHUNK), bool)
        for i in range(CHUNK):
            if odd[i]:
                g[i, blk[i] * h:i + 1] = True
            else:
                g[i, i + 1:(blk[i] + 1) * h] = True
        mats.append(g)
        lmask.append(odd[:, None] & (blk[None, :] == blk[:, None] - 1))
    gstack = np.concatenate(mats, axis=0).astype(np.float32)
    gstack = np.concatenate([gstack] * LOG_TERMS, axis=1)
    return jnp.asarray(gstack, BF16), jnp.asarray(np.stack(lmask), F32)


def _head_params(par, layer, depth):
    lg = [par[l:l + 1, :] for l in range(depth)]
    mx = functools.reduce(jnp.maximum, lg)
    ex = [jnp.exp(v - mx) for v in lg]
    den = functools.reduce(lambda a, b: a + b, ex)
    lb = jnp.zeros_like(mx)
    for l in range(1, layer + 1):
        lb = lb + ex[l] / den
    gh = par[depth:depth + 1, :]
    cw = [par[depth + 1 + j:depth + 2 + j, :] for j in range(CONV_K)]
    gc = par[depth + 1 + CONV_K:depth + 2 + CONV_K, :]
    return lb, gh, cw, gc


def _hgrn_prepare(comp, lb, gstack_ref, nchunk):
    qin = comp(Q)
    f = lb + (1.0 - lb) * _sigmoid(comp(F))
    lf = jnp.log2(f)
    kk_all = 1.0 - f
    lk_all = jnp.log2(jnp.maximum(kk_all, 0.0))
    qs_all = qin * _sigmoid(qin)
    terms, rest = [], lf
    for _ in range(LOG_TERMS):
        terms.append(rest.astype(BF16))
        rest = rest - terms[-1].astype(F32)
    rhs = jnp.concatenate(
        [jnp.concatenate([tm[c * CHUNK:(c + 1) * CHUNK] for tm in terms], axis=0)
         for c in range(nchunk)], axis=1)
    xs_all = jnp.dot(gstack_ref[...], rhs, preferred_element_type=F32)
    return qs_all, kk_all, lk_all, xs_all


def _hgrn_operands(prep, comp, nchunk):
    qs_all, kk_all, lk_all, xs_all = prep
    nblk = CHUNK // DIAG
    tl = lax.broadcasted_iota(jnp.int32, (DIAG, HEAD_DIM), 0)
    row = lax.broadcasted_iota(jnp.int32, (CHUNK, CHUNK), 0)
    dcol = lax.broadcasted_iota(jnp.int32, (CHUNK, CHUNK), 1) - (row - lax.rem(row, DIAG))
    qe_c, vb_c, dec_c, khat_c, w_c, ad_c = [], [], [], [], [], []
    for c in range(nchunk):
        rows = slice(c * CHUNK, (c + 1) * CHUNK)
        lanes = slice(c * HEAD_DIM, (c + 1) * HEAD_DIM)
        qs = qs_all[rows]
        kk = kk_all[rows]
        b = xs_all[0:CHUNK, lanes]
        bl = xs_all[CHUNK:2 * CHUNK, lanes]

        blast = b[CHUNK - 1:CHUNK, :]
        qe_c.append((qs * jnp.exp2(b)).astype(BF16))
        khat_c.append((kk * jnp.exp2(blast - b)).astype(BF16))
        dec_c.append(jnp.exp2(blast))
        vb_c.append(comp(I, rows).astype(BF16))

        ws = []
        for li, h in enumerate(LEVELS):
            xl = xs_all[(2 + li) * CHUNK:(3 + li) * CHUNK, lanes]
            base = jnp.concatenate([(qs if (r // h) % 2 else kk)[r:r + h]
                                    for r in range(0, CHUNK, h)], axis=0)
            ws.append((base * jnp.exp2(xl)).astype(BF16))
        w_c.append(ws)

        bk = bl - lk_all[rows]
        ad = jnp.zeros((CHUNK, CHUNK), F32)
        for s in range(DIAG):
            sums = []
            for blk in range(nblk):
                r = blk * DIAG
                p = jnp.exp2(bl[r:r + DIAG] - bk[r + s:r + s + 1]) * qs[r:r + DIAG]
                p = p if s == 0 else jnp.where(tl >= s, p, 0.0)
                sums.append(jnp.sum(p, axis=-1, keepdims=True))
            ad = jnp.where(dcol == s, jnp.concatenate(sums, axis=0), ad)
        ad_c.append(ad)
    return qe_c, vb_c, dec_c, khat_c, w_c, ad_c


def _hgrn_products(ops, lmask_ref):
    qe_c, vb_c, dec_c, khat_c, w_c, ad_c = ops
    upd_c, a_c = [], []
    for c in range(len(qe_c)):
        upd_c.append(lax.dot_general(vb_c[c], khat_c[c], TN_DIMS, preferred_element_type=F32))
        a = ad_c[c]
        for li, w in enumerate(w_c[c]):
            a = a + lax.dot_general(w, w, NT_DIMS, preferred_element_type=F32) * lmask_ref[li]
        a_c.append(a.astype(BF16))
    return qe_c, vb_c, dec_c, upd_c, a_c


def _hgrn_outputs(scores, st):
    qe_c, vb_c, dec_c, upd_c, a_c = scores
    o_c = []
    for c in range(len(qe_c)):
        o = lax.dot_general(qe_c[c], st.astype(BF16), NT_DIMS, preferred_element_type=F32)
        o_c.append(o + jnp.dot(a_c[c], vb_c[c], preferred_element_type=F32))
        st = st * dec_c[c] + upd_c[c]
    return o_c, st


def _hgrn_finish(o_c, comp, gh, store):
    for c, o in enumerate(o_c):
        rows = slice(c * CHUNK, (c + 1) * CHUNK)
        gin = comp(G, rows)
        y = o * lax.rsqrt(jnp.mean(o * o, axis=-1, keepdims=True) + EPS) * gh
        store(rows, y * (gin * _sigmoid(gin)))


def _short_conv(comp, cw, gc, carry, store):
    u = comp(CC) * comp(CH)
    rows_t = u.shape[0]
    tl = lax.broadcasted_iota(jnp.int32, (8, HEAD_DIM), 0)
    prev1 = carry[7:8, :]
    prev2 = carry[6:7, :]
    u1 = pltpu.roll(u, 1, axis=0)
    u2 = pltpu.roll(u, 2, axis=0)
    u1 = jnp.concatenate([jnp.where(tl == 0, prev1, u1[0:8]), u1[8:]], axis=0)
    u2 = jnp.concatenate([jnp.where(tl == 0, prev2, jnp.where(tl == 1, prev1, u2[0:8])), u2[8:]],
                         axis=0)
    z = comp(CB) * (cw[0] * u2 + cw[1] * u1 + cw[2] * u)
    store(z * lax.rsqrt(jnp.mean(z * z, axis=-1, keepdims=True) + EPS) * gc)
    return u[rows_t - 8:rows_t, :]


def _mixer_kernel(layer, depth, hp, n_j, n_r, x_ref, g_ref, *refs):
    w_refs = refs[:N_COMP]
    (par_ref, gstack_ref, lmask_ref, oh_ref, oc_ref,
     hn_ref, pbuf_a, pbuf_b, st_ref, carry_ref) = refs[N_COMP:]
    n = pl.program_id(0)
    p = jnp.maximum(n - 1, 0)
    head0 = lax.rem(p, n_j) * hp
    rows_t = hn_ref.shape[0]
    nchunk = rows_t // CHUNK
    n_pieces = N_COMP

    @pl.when(n == 0)
    def _():
        pbuf_b[...] = jnp.zeros_like(pbuf_b)

    @pl.when(lax.rem(n, n_j) == 0)
    def _():
        hn_ref[...] = _rms(x_ref[...], g_ref[...]).astype(BF16)

    @pl.when(lax.rem(p // n_j, n_r) == 0)
    def _():
        for hl in range(hp):
            st_ref[head0 + hl] = jnp.zeros((HEAD_DIM, HEAD_DIM), F32)
            carry_ref[head0 + hl] = jnp.zeros((8, HEAD_DIM), F32)

    def step(src_ref, dst_ref):
        pieces = iter(range(n_pieces))

        def project_some(count):
            for k in (next(pieces, None) for _ in range(count)):
                if k is not None:
                    res = jnp.dot(hn_ref[...], w_refs[k][...], preferred_element_type=F32)
                    for hl in range(hp):
                        c0 = (hl * N_COMP + k) * HEAD_DIM
                        dst_ref[:, c0:c0 + HEAD_DIM] = res[:, hl * HEAD_DIM:(hl + 1) * HEAD_DIM]

        def comp_of(hl):
            def comp(k, rows=slice(None)):
                c0 = (hl * N_COMP + k) * HEAD_DIM
                return src_ref[rows, c0:c0 + HEAD_DIM]
            return comp

        def store_of(hl, out_ref):
            def store(*args):
                rows, val = args if len(args) == 2 else (slice(None), args[0])
                out_ref[rows, hl * HEAD_DIM:(hl + 1) * HEAD_DIM] = val.astype(out_ref.dtype)
            return store

        hp_par = [_head_params(par_ref[head0 + hl], layer, depth) for hl in range(hp)]
        prep = [_hgrn_prepare(comp_of(hl), hp_par[hl][0], gstack_ref, nchunk) for hl in range(hp)]
        slots = 2 * hp + 1
        share = [n_pieces // slots + (i < n_pieces % slots) for i in range(slots)]
        project_some(share[0])
        ops = []
        for hl in range(hp):
            ops.append(_hgrn_operands(prep[hl], comp_of(hl), nchunk))
            project_some(share[1 + hl])
        scores = [_hgrn_products(ops[hl], lmask_ref) for hl in range(hp)]
        outs = []
        for hl in range(hp):
            o_c, st = _hgrn_outputs(scores[hl], st_ref[head0 + hl])
            st_ref[head0 + hl] = st
            outs.append(o_c)
            project_some(share[1 + hp + hl])
        for hl in range(hp):
            _hgrn_finish(outs[hl], comp_of(hl), hp_par[hl][1], store_of(hl, oh_ref))
            carry_ref[head0 + hl] = _short_conv(comp_of(hl), hp_par[hl][2], hp_par[hl][3],
                                                carry_ref[head0 + hl], store_of(hl, oc_ref))

    @pl.when(lax.rem(n, 2) == 0)
    def _():
        step(pbuf_b, pbuf_a)

    @pl.when(lax.rem(n, 2) == 1)
    def _():
        step(pbuf_a, pbuf_b)


def _mixer(x2, g, w, par, layer, depth, consts, *, seq, tm, hp):
    m, d = x2.shape
    nh = par.shape[0]
    n_j = nh // hp
    n_r = seq // tm
    tn = hp * N_COMP * HEAD_DIM
    steps = (m // tm) * n_j

    def consumed(n):
        return jnp.maximum(n - 1, 0)

    def produced(n):
        return jnp.minimum(n, steps - 1)

    w_specs = [pl.BlockSpec((None, d, hp * HEAD_DIM),
                            lambda n, k=k: (layer, 0, k * n_j + lax.rem(produced(n), n_j)))
               for k in range(N_COMP)]
    out_sd = jax.ShapeDtypeStruct((m, nh * HEAD_DIM), BF16)
    out_spec = pl.BlockSpec((tm, hp * HEAD_DIM),
                            lambda n: (consumed(n) // n_j, lax.rem(consumed(n), n_j)))
    return pl.pallas_call(
        functools.partial(_mixer_kernel, layer, depth, hp, n_j, n_r),
        out_shape=(out_sd, out_sd),
        grid=(steps + 1,),
        in_specs=[pl.BlockSpec((tm, d), lambda n: (produced(n) // n_j, 0)), _resident(g.shape)]
        + w_specs + [_resident(par.shape)] + [_resident(c.shape) for c in consts],
        out_specs=(out_spec, out_spec),
        scratch_shapes=[pltpu.VMEM((tm, d), BF16), pltpu.VMEM((tm, tn), F32), pltpu.VMEM((tm, tn), F32),
                        pltpu.VMEM((nh, HEAD_DIM, HEAD_DIM), F32), pltpu.VMEM((nh, 8, HEAD_DIM), F32)],
        compiler_params=_params("arbitrary"),
        name="mixer",
    )(x2, g, *([w] * N_COMP), par, *consts)


def _mixout_kernel(oh_ref, oc_ref, x_ref, w_ref, g_ref, o_ref):
    kh = oh_ref.shape[1]
    mix = jnp.dot(oh_ref[...], w_ref[0:kh, :], preferred_element_type=F32)
    mix = mix + jnp.dot(oc_ref[...], w_ref[kh:, :], preferred_element_type=F32)
    o_ref[...] = x_ref[...] + _rms(mix, g_ref[...])


def _mixout(oh, oc, x2, w, g, layer, *, tm):
    m, d = x2.shape
    return pl.pallas_call(
        _mixout_kernel,
        out_shape=jax.ShapeDtypeStruct((m, d), F32),
        grid=(m // tm,),
        in_specs=[pl.BlockSpec((tm, oh.shape[1]), lambda i: (i, 0)),
                  pl.BlockSpec((tm, oc.shape[1]), lambda i: (i, 0)),
                  pl.BlockSpec((tm, d), lambda i: (i, 0)),
                  _layer_resident(w.shape, layer), _resident(g.shape)],
        out_specs=pl.BlockSpec((tm, d), lambda i: (i, 0)),
        compiler_params=_params("arbitrary"),
        name="mixout",
    )(oh, oc, x2, w, g)


def _kv_kernel(m_ref, g_ref, w_ref, o_ref):
    mn = _rms(m_ref[...], g_ref[...]).astype(BF16)
    o_ref[...] = jnp.dot(mn, w_ref[...], preferred_element_type=F32).astype(o_ref.dtype)


def _kvproj(mem2, g, wkv, layer, *, tm):
    m, d = mem2.shape
    return pl.pallas_call(
        _kv_kernel,
        out_shape=jax.ShapeDtypeStruct((2, m, d), BF16),
        grid=(2, m // tm),
        in_specs=[pl.BlockSpec((tm, d), lambda s, i: (i, 0)),
                  pl.BlockSpec((1, d), lambda s, i: (0, 0)),
                  pl.BlockSpec((None, None, d, d), lambda s, i: (layer, s, 0, 0))],
        out_specs=pl.BlockSpec((None, tm, d), lambda s, i: (s, i, 0)),
        compiler_params=_params("arbitrary", "arbitrary"),
        name="kvproj",
    )(mem2, g, wkv)


def _xattn_kernel(x_ref, gpre_ref, wq_ref, k_ref, v_ref, wo_ref, gpost_ref, o_ref):
    x = x_ref[...]
    d = x.shape[1]
    hd = d // XATTN_HEADS
    q = jnp.dot(_rms(x, gpre_ref[...]).astype(BF16), wq_ref[...], preferred_element_type=F32)
    heads = []
    for h in range(XATTN_HEADS):
        cols = slice(h * hd, (h + 1) * hd)
        s = lax.dot_general(q[:, cols].astype(BF16), k_ref[:, cols], NT_DIMS,
                            preferred_element_type=F32) * (hd ** -0.5)
        p = jnp.exp(s - jnp.max(s, axis=-1, keepdims=True))
        p = p / jnp.sum(p, axis=-1, keepdims=True)
        heads.append(jnp.dot(p.astype(BF16), v_ref[:, cols],
                             preferred_element_type=F32).astype(BF16))
    a = jnp.dot(jnp.concatenate(heads, axis=1), wo_ref[...], preferred_element_type=F32)
    o_ref[...] = x + _rms(a, gpost_ref[...])


def _xattn(x3, gpre, wq, kv, wo, gpost, layer, *, tq):
    bsz, seq, d = x3.shape
    nmem = kv.shape[2]
    return pl.pallas_call(
        _xattn_kernel,
        out_shape=jax.ShapeDtypeStruct((bsz, seq, d), F32),
        grid=(bsz, seq // tq),
        in_specs=[pl.BlockSpec((None, tq, d), lambda b, t: (b, t, 0)),
                  _resident(gpre.shape), _layer_resident(wq.shape, layer),
                  pl.BlockSpec((None, None, nmem, d), lambda b, t: (0, b, 0, 0)),
                  pl.BlockSpec((None, None, nmem, d), lambda b, t: (1, b, 0, 0)),
                  _layer_resident(wo.shape, layer), _resident(gpost.shape)],
        out_specs=pl.BlockSpec((None, tq, d), lambda b, t: (b, t, 0)),
        compiler_params=_params("arbitrary", "arbitrary"),
        name="xattn",
    )(x3, gpre, wq, kv, kv, wo, gpost)


def _mlp_kernel(x_ref, gpre_ref, wu_ref, wd_ref, gpost_ref, o_ref, hn_ref):
    j = pl.program_id(1)

    @pl.when(j == 0)
    def _():
        hn_ref[...] = _rms(x_ref[...], gpre_ref[...]).astype(BF16)
        o_ref[...] = jnp.zeros_like(o_ref)

    u = jnp.maximum(jnp.dot(hn_ref[...], wu_ref[...], preferred_element_type=F32), 0.0)
    o_ref[...] += jnp.dot((u * u).astype(BF16), wd_ref[...], preferred_element_type=F32)

    @pl.when(j == pl.num_programs(1) - 1)
    def _():
        o_ref[...] = x_ref[...] + _rms(o_ref[...], gpost_ref[...])


def _mlp(x2, gpre, wu, wd, gpost, layer, *, tm, tf):
    m, d = x2.shape
    ff = wu.shape[2]
    return pl.pallas_call(
        _mlp_kernel,
        out_shape=jax.ShapeDtypeStruct((m, d), F32),
        grid=(m // tm, ff // tf),
        in_specs=[pl.BlockSpec((tm, d), lambda i, j: (i, 0)),
                  _resident(gpre.shape),
                  pl.BlockSpec((None, d, tf), lambda i, j: (layer, 0, j)),
                  pl.BlockSpec((None, tf, d), lambda i, j: (layer, j, 0)),
                  _resident(gpost.shape)],
        out_specs=pl.BlockSpec((tm, d), lambda i, j: (i, 0)),
        scratch_shapes=[pltpu.VMEM((tm, d), BF16)],
        compiler_params=_params("arbitrary", "arbitrary"),
        name="mlp",
    )(x2, gpre, wu, wd, gpost)


def _tile(n, want):
    t = min(n, want)
    while n % t:
        t //= 2
    return t


def _tiles(m, seq, n_mem_rows, n_heads, d_ff):
    return dict(
        mixer=dict(tm=_tile(seq, 512), hp=_tile(n_heads, 2)),
        mixout=dict(tm=_tile(m, 512)),
        kvproj=dict(tm=_tile(n_mem_rows, 1024)),
        xattn=dict(tq=_tile(seq, 512)),
        mlp=dict(tm=_tile(m, 512), tf=_tile(d_ff, 2048)),
    )


def kernel(x, mem, g_mix_pre, w_in, hgrn_lb_logits, hgrn_norm_g, conv_w, conv_norm_g, w_mix_out,
           g_mix_post, g_x_pre, g_mem, w_q, w_k, w_v, w_xo, g_x_post, g_mlp_pre, w_up, w_down,
           g_mlp_post):
    bsz, seq, d = x.shape
    depth = w_in.shape[0]
    nmem = mem.shape[1]
    m = bsz * seq
    nh = hgrn_norm_g.shape[1] // HEAD_DIM
    assert w_in.shape[2] == N_COMP * nh * HEAD_DIM and conv_norm_g.shape[1] == nh * HEAD_DIM

    w_in_b = w_in.astype(BF16)
    w_mix_b = w_mix_out.astype(BF16)
    w_q_b = w_q.astype(BF16)
    w_kv_b = jnp.stack([w_k, w_v], axis=1).astype(BF16)
    w_xo_b = w_xo.astype(BF16)
    w_up_b = w_up.astype(BF16)
    w_down_b = w_down.astype(BF16)
    consts = _hgrn_constants()

    def row(p, l):
        return p[l][None, :]

    t = _tiles(m, seq, bsz * nmem, nh, w_up.shape[2])
    mem2 = mem.reshape(bsz * nmem, d)
    x2 = x.reshape(m, d)
    for l in range(depth):
        par = jnp.concatenate([hgrn_lb_logits, row(hgrn_norm_g, l), conv_w[l], row(conv_norm_g, l)],
                              axis=0)
        par = par.reshape(par.shape[0], nh, HEAD_DIM).transpose(1, 0, 2)
        oh, oc = _mixer(x2, row(g_mix_pre, l), w_in_b, par, l, depth, consts, seq=seq, **t["mixer"])
        x2 = _mixout(oh, oc, x2, w_mix_b, row(g_mix_post, l), l, **t["mixout"])
        kv = _kvproj(mem2, row(g_mem, l), w_kv_b, l, **t["kvproj"])
        x3 = _xattn(x2.reshape(bsz, seq, d), row(g_x_pre, l), w_q_b,
                    kv.reshape(2, bsz, nmem, d), w_xo_b, row(g_x_post, l), l, **t["xattn"])
        x2 = _mlp(x3.reshape(m, d), row(g_mlp_pre, l), w_up_b, w_down_b, row(g_mlp_post, l), l,
                  **t["mlp"])
    return x2.reshape(bsz, seq, d)
```

```python
import functools

import numpy as np
import jax
import jax.numpy as jnp
from jax import lax
from jax.experimental import pallas as pl
from jax.experimental.pallas import tpu as pltpu

F32 = jnp.float32
BF16 = jnp.bfloat16

EPS = 1e-6
HEAD_DIM = 128
CHUNK = 64
DIAG = 8
LEVELS = tuple(h for h in (32, 16, 8) if h >= DIAG)
LOG_TERMS = 2
CONV_K = 3
XATTN_HEADS = 4
Q, F, I, G, CB, CC, CH = range(7)
N_COMP = 7
V7X_VMEM_LIMIT_BYTES = 60 * 1024 * 1024

NT_DIMS = (((1,), (1,)), ((), ()))
TN_DIMS = (((0,), (0,)), ((), ()))


def _params(*semantics):
    return pltpu.CompilerParams(dimension_semantics=semantics,
                                vmem_limit_bytes=V7X_VMEM_LIMIT_BYTES)


def _rms(x, g):
    return x * lax.rsqrt(jnp.mean(x * x, axis=-1, keepdims=True) + EPS) * g


def _sigmoid(x):
    return 1.0 / (1.0 + jnp.exp(-x))


def _resident(shape):
    return pl.BlockSpec(shape, lambda *_: (0,) * len(shape), pipeline_mode=pl.Buffered(1))


def _layer_resident(shape, layer):
    return pl.BlockSpec((None,) + tuple(shape[1:]), lambda *_: (layer,) + (0,) * (len(shape) - 1),
                        pipeline_mode=pl.Buffered(1))


def _hgrn_constants():
    t = np.arange(CHUNK)
    tri = (t[None, :] <= t[:, None])
    gloc = tri & (t[None, :] // DIAG == t[:, None] // DIAG)
    mats = [tri, gloc]
    lmask = []
    for h in LEVELS:
        blk = t // h
        odd = (blk % 2 == 1)
        g = np.zeros((CHUNK, CHUNK), bool)
        for i in range(CHUNK):
            if odd[i]:
                g[i, blk[i] * h:i + 1] = True
            else:
                g[i, i + 1:(blk[i] + 1) * h] = True
        mats.append(g)
        lmask.append(odd[:, None] & (blk[None, :] == blk[:, None] - 1))
    gstack = np.concatenate(mats, axis=0).astype(np.float32)
    gstack = np.concatenate([gstack] * LOG_TERMS, axis=1)
    return jnp.asarray(gstack, BF16), jnp.asarray(np.stack(lmask), F32)


def _head_params(par, layer, depth):
    lg = [par[l:l + 1, :] for l in range(depth)]
    mx = functools.reduce(jnp.maximum, lg)
    ex = [jnp.exp(v - mx) for v in lg]
    den = functools.reduce(lambda a, b: a + b, ex)
    lb = jnp.zeros_like(mx)
    for l in range(1, layer + 1):
        lb = lb + ex[l] / den
    gh = par[depth:depth + 1, :]
    cw = [par[depth + 1 + j:depth + 2 + j, :] for j in range(CONV_K)]
    gc = par[depth + 1 + CONV_K:depth + 2 + CONV_K, :]
    return lb, gh, cw, gc


def _hgrn_prepare(comp, lb, gstack_ref, nchunk):
    qin = comp(Q)
    f = lb + (1.0 - lb) * _sigmoid(comp(F))
    lf = jnp.log2(f)
    kk_all = 1.0 - f
    lk_all = jnp.log2(jnp.maximum(kk_all, 0.0))
    qs_all = qin * _sigmoid(qin)
    terms, rest = [], lf
    for _ in range(LOG_TERMS):
        terms.append(rest.astype(BF16))
        rest = rest - terms[-1].astype(F32)
    rhs = jnp.concatenate(
        [jnp.concatenate([tm[c * CHUNK:(c + 1) * CHUNK] for tm in terms], axis=0)
         for c in range(nchunk)], axis=1)
    xs_all = jnp.dot(gstack_ref[...], rhs, preferred_element_type=F32)
    return qs_all, kk_all, lk_all, xs_all


def _hgrn_operands(prep, comp, nchunk):
    qs_all, kk_all, lk_all, xs_all = prep
    nblk = CHUNK // DIAG
    tl = lax.broadcasted_iota(jnp.int32, (DIAG, HEAD_DIM), 0)
    row = lax.broadcasted_iota(jnp.int32, (CHUNK, CHUNK), 0)
    dcol = lax.broadcasted_iota(jnp.int32, (CHUNK, CHUNK), 1) - (row - lax.rem(row, DIAG))
    qe_c, vb_c, dec_c, khat_c, w_c, ad_c = [], [], [], [], [], []
    for c in range(nchunk):
        rows = slice(c * CHUNK, (c + 1) * CHUNK)
        lanes = slice(c * HEAD_DIM, (c + 1) * HEAD_DIM)
        qs = qs_all[rows]
        kk = kk_all[rows]
        b = xs_all[0:CHUNK, lanes]
        bl = xs_all[CHUNK:2 * CHUNK, lanes]

        blast = b[CHUNK - 1:CHUNK, :]
        qe_c.append((qs * jnp.exp2(b)).astype(BF16))
        khat_c.append((kk * jnp.exp2(blast - b)).astype(BF16))
        dec_c.append(jnp.exp2(blast))
        vb_c.append(comp(I, rows).astype(BF16))

        ws = []
        for li, h in enumerate(LEVELS):
            xl = xs_all[(2 + li) * CHUNK:(3 + li) * CHUNK, lanes]
            base = jnp.concatenate([(qs if (r // h) % 2 else kk)[r:r + h]
                                    for r in range(0, CHUNK, h)], axis=0)
            ws.append((base * jnp.exp2(xl)).astype(BF16))
        w_c.append(ws)

        bk = bl - lk_all[rows]
        ad = jnp.zeros((CHUNK, CHUNK), F32)
        for s in range(DIAG):
            sums = []
            for blk in range(nblk):
                r = blk * DIAG
                p = jnp.exp2(bl[r:r + DIAG] - bk[r + s:r + s + 1]) * qs[r:r + DIAG]
                p = p if s == 0 else jnp.where(tl >= s, p, 0.0)
                sums.append(jnp.sum(p, axis=-1, keepdims=True))
            ad = jnp.where(dcol == s, jnp.concatenate(sums, axis=0), ad)
        ad_c.append(ad)
    return qe_c, vb_c, dec_c, khat_c, w_c, ad_c


def _hgrn_products(ops, lmask_ref):
    qe_c, vb_c, dec_c, khat_c, w_c, ad_c = ops
    upd_c, a_c = [], []
    for c in range(len(qe_c)):
        upd_c.append(lax.dot_general(vb_c[c], khat_c[c], TN_DIMS, preferred_element_type=F32))
        a = ad_c[c]
        for li, w in enumerate(w_c[c]):
            a = a + lax.dot_general(w, w, NT_DIMS, preferred_element_type=F32) * lmask_ref[li]
        a_c.append(a.astype(BF16))
    return qe_c, vb_c, dec_c, upd_c, a_c


def _hgrn_outputs(scores, st):
    qe_c, vb_c, dec_c, upd_c, a_c = scores
    o_c = []
    for c in range(len(qe_c)):
        o = lax.dot_general(qe_c[c], st.astype(BF16), NT_DIMS, preferred_element_type=F32)
        o_c.append(o + jnp.dot(a_c[c], vb_c[c], preferred_element_type=F32))
        st = st * dec_c[c] + upd_c[c]
    return o_c, st


def _hgrn_finish(o_c, comp, gh, store):
    for c, o in enumerate(o_c):
        rows = slice(c * CHUNK, (c + 1) * CHUNK)
        gin = comp(G, rows)
        y = o * lax.rsqrt(jnp.mean(o * o, axis=-1, keepdims=True) + EPS) * gh
        store(rows, y * (gin * _sigmoid(gin)))


def _short_conv(comp, cw, gc, carry, store):
    u = comp(CC) * comp(CH)
    rows_t = u.shape[0]
    tl = lax.broadcasted_iota(jnp.int32, (8, HEAD_DIM), 0)
    prev1 = carry[7:8, :]
    prev2 = carry[6:7, :]
    u1 = pltpu.roll(u, 1, axis=0)
    u2 = pltpu.roll(u, 2, axis=0)
    u1 = jnp.concatenate([jnp.where(tl == 0, prev1, u1[0:8]), u1[8:]], axis=0)
    u2 = jnp.concatenate([jnp.where(tl == 0, prev2, jnp.where(tl == 1, prev1, u2[0:8])), u2[8:]],
                         axis=0)
    z = comp(CB) * (cw[0] * u2 + cw[1] * u1 + cw[2] * u)
    store(z * lax.rsqrt(jnp.mean(z * z, axis=-1, keepdims=True) + EPS) * gc)
    return u[rows_t - 8:rows_t, :]


def _mixer_kernel(layer, depth, hp, n_j, n_r, x_ref, g_ref, *refs):
    w_refs = refs[:N_COMP]
    (par_ref, gstack_ref, lmask_ref, oh_ref, oc_ref,
     hn_ref, pbuf_a, pbuf_b, st_ref, carry_ref) = refs[N_COMP:]
    n = pl.program_id(0)
    p = jnp.maximum(n - 1, 0)
    head0 = lax.rem(p, n_j) * hp
    rows_t = hn_ref.shape[0]
    nchunk = rows_t // CHUNK
    n_pieces = N_COMP

    @pl.when(n == 0)
    def _():
        pbuf_b[...] = jnp.zeros_like(pbuf_b)

    @pl.when(lax.rem(n, n_j) == 0)
    def _():
        hn_ref[...] = _rms(x_ref[...], g_ref[...]).astype(BF16)

    @pl.when(lax.rem(p // n_j, n_r) == 0)
    def _():
        for hl in range(hp):
            st_ref[head0 + hl] = jnp.zeros((HEAD_DIM, HEAD_DIM), F32)
            carry_ref[head0 + hl] = jnp.zeros((8, HEAD_DIM), F32)

    def step(src_ref, dst_ref):
        pieces = iter(range(n_pieces))

        def project_some(count):
            for k in (next(pieces, None) for _ in range(count)):
                if k is not None:
                    res = jnp.dot(hn_ref[...], w_refs[k][...], preferred_element_type=F32)
                    for hl in range(hp):
                        c0 = (hl * N_COMP + k) * HEAD_DIM
                        dst_ref[:, c0:c0 + HEAD_DIM] = res[:, hl * HEAD_DIM:(hl + 1) * HEAD_DIM]

        def comp_of(hl):
            def comp(k, rows=slice(None)):
                c0 = (hl * N_COMP + k) * HEAD_DIM
                return src_ref[rows, c0:c0 + HEAD_DIM]
            return comp

        def store_of(hl, out_ref):
            def store(*args):
                rows, val = args if len(args) == 2 else (slice(None), args[0])
                out_ref[rows, hl * HEAD_DIM:(hl + 1) * HEAD_DIM] = val.astype(out_ref.dtype)
            return store

        hp_par = [_head_params(par_ref[head0 + hl], layer, depth) for hl in range(hp)]
        prep = [_hgrn_prepare(comp_of(hl), hp_par[hl][0], gstack_ref, nchunk) for hl in range(hp)]
        slots = 2 * hp + 1
        share = [n_pieces // slots + (i < n_pieces % slots) for i in range(slots)]
        project_some(share[0])
        ops = []
        for hl in range(hp):
            ops.append(_hgrn_operands(prep[hl], comp_of(hl), nchunk))
            project_some(share[1 + hl])
        scores = [_hgrn_products(ops[hl], lmask_ref) for hl in range(hp)]
        outs = []
        for hl in range(hp):
            o_c, st = _hgrn_outputs(scores[hl], st_ref[head0 + hl])
            st_ref[head0 + hl] = st
            outs.append(o_c)
            project_some(share[1 + hp + hl])
        for hl in range(hp):
            _hgrn_finish(outs[hl], comp_of(hl), hp_par[hl][1], store_of(hl, oh_ref))
            carry_ref[head0 + hl] = _short_conv(comp_of(hl), hp_par[hl][2], hp_par[hl][3],
                                                carry_ref[head0 + hl], store_of(hl, oc_ref))

    @pl.when(lax.rem(n, 2) == 0)
    def _():
        step(pbuf_b, pbuf_a)

    @pl.when(lax.rem(n, 2) == 1)
    def _():
        step(pbuf_a, pbuf_b)


def _mixer(x2, g, w, par, layer, depth, consts, *, seq, tm, hp):
    m, d = x2.shape
    nh = par.shape[0]
    n_j = nh // hp
    n_r = seq // tm
    tn = hp * N_COMP * HEAD_DIM
    steps = (m // tm) * n_j

    def consumed(n):
        return jnp.maximum(n - 1, 0)

    def produced(n):
        return jnp.minimum(n, steps - 1)

    w_specs = [pl.BlockSpec((None, d, hp * HEAD_DIM),
                            lambda n, k=k: (layer, 0, k * n_j + lax.rem(produced(n), n_j)))
               for k in range(N_COMP)]
    out_sd = jax.ShapeDtypeStruct((m, nh * HEAD_DIM), BF16)
    out_spec = pl.BlockSpec((tm, hp * HEAD_DIM),
                            lambda n: (consumed(n) // n_j, lax.rem(consumed(n), n_j)))
    return pl.pallas_call(
        functools.partial(_mixer_kernel, layer, depth, hp, n_j, n_r),
        out_shape=(out_sd, out_sd),
        grid=(steps + 1,),
        in_specs=[pl.BlockSpec((tm, d), lambda n: (produced(n) // n_j, 0)), _resident(g.shape)]
        + w_specs + [_resident(par.shape)] + [_resident(c.shape) for c in consts],
        out_specs=(out_spec, out_spec),
        scratch_shapes=[pltpu.VMEM((tm, d), BF16), pltpu.VMEM((tm, tn), F32), pltpu.VMEM((tm, tn), F32),
                        pltpu.VMEM((nh, HEAD_DIM, HEAD_DIM), F32), pltpu.VMEM((nh, 8, HEAD_DIM), F32)],
        compiler_params=_params("arbitrary"),
        name="mixer",
    )(x2, g, *([w] * N_COMP), par, *consts)


def _mixout_kernel(oh_ref, oc_ref, x_ref, w_ref, g_ref, o_ref):
    kh = oh_ref.shape[1]
    mix = jnp.dot(oh_ref[...], w_ref[0:kh, :], preferred_element_type=F32)
    mix = mix + jnp.dot(oc_ref[...], w_ref[kh:, :], preferred_element_type=F32)
    o_ref[...] = x_ref[...] + _rms(mix, g_ref[...])


def _mixout(oh, oc, x2, w, g, layer, *, tm):
    m, d = x2.shape
    return pl.pallas_call(
        _mixout_kernel,
        out_shape=jax.ShapeDtypeStruct((m, d), F32),
        grid=(m // tm,),
        in_specs=[pl.BlockSpec((tm, oh.shape[1]), lambda i: (i, 0)),
                  pl.BlockSpec((tm, oc.shape[1]), lambda i: (i, 0)),
                  pl.BlockSpec((tm, d), lambda i: (i, 0)),
                  _layer_resident(w.shape, layer), _resident(g.shape)],
        out_specs=pl.BlockSpec((tm, d), lambda i: (i, 0)),
        compiler_params=_params("arbitrary"),
        name="mixout",
    )(oh, oc, x2, w, g)


def _kv_kernel(m_ref, g_ref, w_ref, o_ref):
    mn = _rms(m_ref[...], g_ref[...]).astype(BF16)
    o_ref[...] = jnp.dot(mn, w_ref[...], preferred_element_type=F32).astype(o_ref.dtype)


def _kvproj(mem2, g, wkv, layer, *, tm):
    m, d = mem2.shape
    return pl.pallas_call(
        _kv_kernel,
        out_shape=jax.ShapeDtypeStruct((2, m, d), BF16),
        grid=(2, m // tm),
        in_specs=[pl.BlockSpec((tm, d), lambda s, i: (i, 0)),
                  pl.BlockSpec((1, d), lambda s, i: (0, 0)),
                  pl.BlockSpec((None, None, d, d), lambda s, i: (layer, s, 0, 0))],
        out_specs=pl.BlockSpec((None, tm, d), lambda s, i: (s, i, 0)),
        compiler_params=_params("arbitrary", "arbitrary"),
        name="kvproj",
    )(mem2, g, wkv)


def _xattn_kernel(x_ref, gpre_ref, wq_ref, k_ref, v_ref, wo_ref, gpost_ref, o_ref):
    x = x_ref[...]
    d = x.shape[1]
    hd = d // XATTN_HEADS
    q = jnp.dot(_rms(x, gpre_ref[...]).astype(BF16), wq_ref[...], preferred_element_type=F32)
    heads = []
    for h in range(XATTN_HEADS):
        cols = slice(h * hd, (h + 1) * hd)
        s = lax.dot_general(q[:, cols].astype(BF16), k_ref[:, cols], NT_DIMS,
                            preferred_element_type=F32) * (hd ** -0.5)
        p = jnp.exp(s - jnp.max(s, axis=-1, keepdims=True))
        p = p / jnp.sum(p, axis=-1, keepdims=True)
        heads.append(jnp.dot(p.astype(BF16), v_ref[:, cols],
                             preferred_element_type=F32).astype(BF16))
    a = jnp.dot(jnp.concatenate(heads, axis=1), wo_ref[...], preferred_element_type=F32)
    o_ref[...] = x + _rms(a, gpost_ref[...])


def _xattn(x3, gpre, wq, kv, wo, gpost, layer, *, tq):
    bsz, seq, d = x3.shape
    nmem = kv.shape[2]
    return pl.pallas_call(
        _xattn_kernel,
        out_shape=jax.ShapeDtypeStruct((bsz, seq, d), F32),
        grid=(bsz, seq // tq),
        in_specs=[pl.BlockSpec((None, tq, d), lambda b, t: (b, t, 0)),
                  _resident(gpre.shape), _layer_resident(wq.shape, layer),
                  pl.BlockSpec((None, None, nmem, d), lambda b, t: (0, b, 0, 0)),
                  pl.BlockSpec((None, None, nmem, d), lambda b, t: (1, b, 0, 0)),
                  _layer_resident(wo.shape, layer), _resident(gpost.shape)],
        out_specs=pl.BlockSpec((None, tq, d), lambda b, t: (b, t, 0)),
        compiler_params=_params("arbitrary", "arbitrary"),
        name="xattn",
    )(x3, gpre, wq, kv, kv, wo, gpost)


def _mlp_kernel(x_ref, gpre_ref, wu_ref, wd_ref, gpost_ref, o_ref, hn_ref):
    j = pl.program_id(1)
    last = pl.num_programs(1) - 1
    half = o_ref.shape[0] // 2
    halves = (slice(0, half), slice(half, 2 * half))

    def up(rows):
        u = jnp.maximum(jnp.dot(hn_ref[rows, :], wu_ref[...], preferred_element_type=F32), 0.0)
        return (u * u).astype(BF16)

    @pl.when(j == 0)
    def _():
        for rows in halves:
            hn_ref[rows, :] = _rms(x_ref[rows, :], gpre_ref[...]).astype(BF16)
        uu = [up(rows) for rows in halves]
        for rows, u in zip(halves, uu):
            o_ref[rows, :] = jnp.dot(u, wd_ref[...], preferred_element_type=F32)

    @pl.when(jnp.logical_and(j > 0, j < last))
    def _():
        o_ref[...] += jnp.dot(up(slice(None)), wd_ref[...], preferred_element_type=F32)

    @pl.when(j == last)
    def _():
        uu = [up(rows) for rows in halves]
        for rows, u in zip(halves, uu):
            acc = o_ref[rows, :] + jnp.dot(u, wd_ref[...], preferred_element_type=F32)
            o_ref[rows, :] = x_ref[rows, :] + _rms(acc, gpost_ref[...])


def _mlp(x2, gpre, wu, wd, gpost, layer, *, tm, tf):
    m, d = x2.shape
    ff = wu.shape[2]
    assert ff // tf >= 2, "the kernel's first and last hidden blocks must be different steps"
    return pl.pallas_call(
        _mlp_kernel,
        out_shape=jax.ShapeDtypeStruct((m, d), F32),
        grid=(m // tm, ff // tf),
        in_specs=[pl.BlockSpec((tm, d), lambda i, j: (i, 0)),
                  _resident(gpre.shape),
                  pl.BlockSpec((None, d, tf), lambda i, j: (layer, 0, j)),
                  pl.BlockSpec((None, tf, d), lambda i, j: (layer, j, 0)),
                  _resident(gpost.shape)],
        out_specs=pl.BlockSpec((tm, d), lambda i, j: (i, 0)),
        scratch_shapes=[pltpu.VMEM((tm, d), BF16)],
        compiler_params=_params("arbitrary", "arbitrary"),
        name="mlp",
    )(x2, gpre, wu, wd, gpost)


def _tile(n, want):
    t = min(n, want)
    while n % t:
        t //= 2
    return t


def _tiles(m, seq, n_mem_rows, n_heads, d_ff):
    return dict(
        mixer=dict(tm=_tile(seq, 512), hp=_tile(n_heads, 2)),
        mixout=dict(tm=_tile(m, 512)),
        kvproj=dict(tm=_tile(n_mem_rows, 1024)),
        xattn=dict(tq=_tile(seq, 512)),
        mlp=dict(tm=_tile(m, 512), tf=_tile(d_ff, 2048)),
    )


def kernel(x, mem, g_mix_pre, w_in, hgrn_lb_logits, hgrn_norm_g, conv_w, conv_norm_g, w_mix_out,
           g_mix_post, g_x_pre, g_mem, w_q, w_k, w_v, w_xo, g_x_post, g_mlp_pre, w_up, w_down,
           g_mlp_post):
    bsz, seq, d = x.shape
    depth = w_in.shape[0]
    nmem = mem.shape[1]
    m = bsz * seq
    nh = hgrn_norm_g.shape[1] // HEAD_DIM
    assert w_in.shape[2] == N_COMP * nh * HEAD_DIM and conv_norm_g.shape[1] == nh * HEAD_DIM

    w_in_b = w_in.astype(BF16)
    w_mix_b = w_mix_out.astype(BF16)
    w_q_b = w_q.astype(BF16)
    w_kv_b = jnp.stack([w_k, w_v], axis=1).astype(BF16)
    w_xo_b = w_xo.astype(BF16)
    w_up_b = w_up.astype(BF16)
    w_down_b = w_down.astype(BF16)
    consts = _hgrn_constants()

    def row(p, l):
        return p[l][None, :]

    t = _tiles(m, seq, bsz * nmem, nh, w_up.shape[2])
    mem2 = mem.reshape(bsz * nmem, d)
    x2 = x.reshape(m, d)
    for l in range(depth):
        par = jnp.concatenate([hgrn_lb_logits, row(hgrn_norm_g, l), conv_w[l], row(conv_norm_g, l)],
                              axis=0)
        par = par.reshape(par.shape[0], nh, HEAD_DIM).transpose(1, 0, 2)
        oh, oc = _mixer(x2, row(g_mix_pre, l), w_in_b, par, l, depth, consts, seq=seq, **t["mixer"])
        x2 = _mixout(oh, oc, x2, w_mix_b, row(g_mix_post, l), l, **t["mixout"])
        kv = _kvproj(mem2, row(g_mem, l), w_kv_b, l, **t["kvproj"])
        x3 = _xattn(x2.reshape(bsz, seq, d), row(g_x_pre, l), w_q_b,
                    kv.reshape(2, bsz, nmem, d), w_xo_b, row(g_x_post, l), l, **t["xattn"])
        x2 = _mlp(x3.reshape(m, d), row(g_mlp_pre, l), w_up_b, w_down_b, row(g_mlp_post, l), l,
                  **t["mlp"])
    return x2.reshape(bsz, seq, d)
```

```python
import functools

import numpy as np
import jax
import jax.numpy as jnp
from jax import lax
from jax.experimental import pallas as pl
from jax.experimental.pallas import tpu as pltpu

F32 = jnp.float32
BF16 = jnp.bfloat16

EPS = 1e-6
HEAD_DIM = 128
CHUNK = 64
DIAG = 8
LEVELS = tuple(h for h in (32, 16, 8) if h >= DIAG)
LOG_TERMS = 2
CONV_K = 3
XATTN_HEADS = 4
Q, F, I, G, CB, CC, CH = range(7)
N_COMP = 7
V7X_VMEM_LIMIT_BYTES = 60 * 1024 * 1024

NT_DIMS = (((1,), (1,)), ((), ()))
TN_DIMS = (((0,), (0,)), ((), ()))


def _params(*semantics):
    return pltpu.CompilerParams(dimension_semantics=semantics,
                                vmem_limit_bytes=V7X_VMEM_LIMIT_BYTES)


def _rms(x, g):
    return x * lax.rsqrt(jnp.mean(x * x, axis=-1, keepdims=True) + EPS) * g


def _sigmoid(x):
    return 1.0 / (1.0 + jnp.exp(-x))


def _resident(shape):
    return pl.BlockSpec(shape, lambda *_: (0,) * len(shape), pipeline_mode=pl.Buffered(1))


def _layer_resident(shape, layer):
    return pl.BlockSpec((None,) + tuple(shape[1:]), lambda *_: (layer,) + (0,) * (len(shape) - 1),
                        pipeline_mode=pl.Buffered(1))


def _hgrn_constants():
    t = np.arange(CHUNK)
    mats = [t[None, :] <= t[:, None]]
    lmask = []
    for h in LEVELS:
        blk = t // h
        odd = (blk % 2 == 1)
        g = np.zeros((CHUNK, CHUNK), bool)
        for i in range(CHUNK):
            if odd[i]:
                g[i, blk[i] * h:i + 1] = True
            else:
                g[i, i + 1:(blk[i] + 1) * h] = True
        if h > DIAG:
            mats.append(g)
        lmask.append(odd[:, None] & (blk[None, :] == blk[:, None] - 1))
    gstack = np.concatenate(mats, axis=0).astype(np.float32)
    gstack = np.concatenate([gstack] * LOG_TERMS, axis=1)
    return jnp.asarray(gstack, BF16), jnp.asarray(np.stack(lmask), F32)


def _head_params(par, layer, depth):
    lg = [par[l:l + 1, :] for l in range(depth)]
    mx = functools.reduce(jnp.maximum, lg)
    ex = [jnp.exp(v - mx) for v in lg]
    den = functools.reduce(lambda a, b: a + b, ex)
    lb = jnp.zeros_like(mx)
    for l in range(1, layer + 1):
        lb = lb + ex[l] / den
    gh = par[depth:depth + 1, :]
    cw = [par[depth + 1 + j:depth + 2 + j, :] for j in range(CONV_K)]
    gc = par[depth + 1 + CONV_K:depth + 2 + CONV_K, :]
    return lb, gh, cw, gc


def _hgrn_prepare(comp, lb, gstack_ref, nchunk):
    qin = comp(Q)
    f = lb + (1.0 - lb) * _sigmoid(comp(F))
    lf = jnp.log2(f)
    kk_all = 1.0 - f
    lk_all = jnp.log2(jnp.maximum(kk_all, 0.0))
    qs_all = qin * _sigmoid(qin)
    terms, rest = [], lf
    for _ in range(LOG_TERMS):
        terms.append(rest.astype(BF16))
        rest = rest - terms[-1].astype(F32)
    rhs = jnp.concatenate(
        [jnp.concatenate([tm[c * CHUNK:(c + 1) * CHUNK] for tm in terms], axis=0)
         for c in range(nchunk)], axis=1)
    xs_all = jnp.dot(gstack_ref[...], rhs, preferred_element_type=F32)
    return qs_all, kk_all, lk_all, xs_all


def _hgrn_operands(prep, comp, nchunk):
    qs_all, kk_all, lk_all, xs_all = prep
    nblk = CHUNK // DIAG
    tl = lax.broadcasted_iota(jnp.int32, (DIAG, HEAD_DIM), 0)
    row = lax.broadcasted_iota(jnp.int32, (CHUNK, CHUNK), 0)
    dcol = lax.broadcasted_iota(jnp.int32, (CHUNK, CHUNK), 1) - (row - lax.rem(row, DIAG))
    qe_c, vb_c, dec_c, khat_c, w_c, ad_c = [], [], [], [], [], []
    for c in range(nchunk):
        rows = slice(c * CHUNK, (c + 1) * CHUNK)
        lanes = slice(c * HEAD_DIM, (c + 1) * HEAD_DIM)
        qs = qs_all[rows]
        kk = kk_all[rows]
        b = xs_all[0:CHUNK, lanes]
        bl = jnp.concatenate([b[0:DIAG]] + [b[r:r + DIAG] - b[r - 1:r]
                                             for r in range(DIAG, CHUNK, DIAG)], axis=0)

        blast = b[CHUNK - 1:CHUNK, :]
        qe_c.append((qs * jnp.exp2(b)).astype(BF16))
        khat_c.append((kk * jnp.exp2(blast - b)).astype(BF16))
        dec_c.append(jnp.exp2(blast))
        vb_c.append(comp(I, rows).astype(BF16))

        ws = []
        for li, h in enumerate(LEVELS):
            if h > DIAG:
                xl = xs_all[(1 + li) * CHUNK:(2 + li) * CHUNK, lanes]
            else:
                xl = jnp.concatenate([bl[r:r + h] if (r // h) % 2 else bl[r + h - 1:r + h] - bl[r:r + h]
                                      for r in range(0, CHUNK, h)], axis=0)
            base = jnp.concatenate([(qs if (r // h) % 2 else kk)[r:r + h]
                                    for r in range(0, CHUNK, h)], axis=0)
            ws.append((base * jnp.exp2(xl)).astype(BF16))
        w_c.append(ws)

        bk = bl - lk_all[rows]
        ad = jnp.zeros((CHUNK, CHUNK), F32)
        for s in range(DIAG):
            sums = []
            for blk in range(nblk):
                r = blk * DIAG
                p = jnp.exp2(bl[r:r + DIAG] - bk[r + s:r + s + 1]) * qs[r:r + DIAG]
                p = p if s == 0 else jnp.where(tl >= s, p, 0.0)
                sums.append(jnp.sum(p, axis=-1, keepdims=True))
            ad = jnp.where(dcol == s, jnp.concatenate(sums, axis=0), ad)
        ad_c.append(ad)
    return qe_c, vb_c, dec_c, khat_c, w_c, ad_c


def _hgrn_products(ops, lmask_ref):
    qe_c, vb_c, dec_c, khat_c, w_c, ad_c = ops
    upd_c, a_c = [], []
    for c in range(len(qe_c)):
        upd_c.append(lax.dot_general(vb_c[c], khat_c[c], TN_DIMS, preferred_element_type=F32))
        a = ad_c[c]
        for li, w in enumerate(w_c[c]):
            a = a + lax.dot_general(w, w, NT_DIMS, preferred_element_type=F32) * lmask_ref[li]
        a_c.append(a.astype(BF16))
    return qe_c, vb_c, dec_c, upd_c, a_c


def _hgrn_outputs(scores, st):
    qe_c, vb_c, dec_c, upd_c, a_c = scores
    o_c = []
    for c in range(len(qe_c)):
        o = lax.dot_general(qe_c[c], st.astype(BF16), NT_DIMS, preferred_element_type=F32)
        o_c.append(o + jnp.dot(a_c[c], vb_c[c], preferred_element_type=F32))
        st = st * dec_c[c] + upd_c[c]
    return o_c, st


def _hgrn_finish(o_c, comp, gh, store):
    for c, o in enumerate(o_c):
        rows = slice(c * CHUNK, (c + 1) * CHUNK)
        gin = comp(G, rows)
        y = o * lax.rsqrt(jnp.mean(o * o, axis=-1, keepdims=True) + EPS) * gh
        store(rows, y * (gin * _sigmoid(gin)))


def _short_conv(comp, cw, gc, carry, store):
    u = comp(CC) * comp(CH)
    rows_t = u.shape[0]
    tl = lax.broadcasted_iota(jnp.int32, (8, HEAD_DIM), 0)
    prev1 = carry[7:8, :]
    prev2 = carry[6:7, :]
    u1 = pltpu.roll(u, 1, axis=0)
    u2 = pltpu.roll(u, 2, axis=0)
    u1 = jnp.concatenate([jnp.where(tl == 0, prev1, u1[0:8]), u1[8:]], axis=0)
    u2 = jnp.concatenate([jnp.where(tl == 0, prev2, jnp.where(tl == 1, prev1, u2[0:8])), u2[8:]],
                         axis=0)
    z = comp(CB) * (cw[0] * u2 + cw[1] * u1 + cw[2] * u)
    store(z * lax.rsqrt(jnp.mean(z * z, axis=-1, keepdims=True) + EPS) * gc)
    return u[rows_t - 8:rows_t, :]


def _mixer_kernel(layer, depth, hp, n_j, n_r, x_ref, g_ref, *refs):
    w_refs = refs[:N_COMP]
    (par_ref, gstack_ref, lmask_ref, oh_ref, oc_ref,
     hn_ref, pbuf_a, pbuf_b, st_ref, carry_ref) = refs[N_COMP:]
    n = pl.program_id(0)
    p = jnp.maximum(n - 1, 0)
    head0 = lax.rem(p, n_j) * hp
    rows_t = hn_ref.shape[0]
    nchunk = rows_t // CHUNK
    n_pieces = N_COMP

    @pl.when(n == 0)
    def _():
        pbuf_b[...] = jnp.zeros_like(pbuf_b)

    @pl.when(lax.rem(p // n_j, n_r) == 0)
    def _():
        for hl in range(hp):
            st_ref[head0 + hl] = jnp.zeros((HEAD_DIM, HEAD_DIM), F32)
            carry_ref[head0 + hl] = jnp.zeros((8, HEAD_DIM), F32)

    def step(src_ref, dst_ref, new_rows):
        if new_rows:
            hn_ref[...] = _rms(x_ref[...], g_ref[...]).astype(BF16)
        pieces = iter(range(n_pieces))

        def project_some(count):
            for k in (next(pieces, None) for _ in range(count)):
                if k is not None:
                    res = jnp.dot(hn_ref[...], w_refs[k][...], preferred_element_type=F32)
                    for hl in range(hp):
                        c0 = (hl * N_COMP + k) * HEAD_DIM
                        dst_ref[:, c0:c0 + HEAD_DIM] = res[:, hl * HEAD_DIM:(hl + 1) * HEAD_DIM]

        def comp_of(hl):
            def comp(k, rows=slice(None)):
                c0 = (hl * N_COMP + k) * HEAD_DIM
                return src_ref[rows, c0:c0 + HEAD_DIM]
            return comp

        def store_of(hl, out_ref):
            def store(*args):
                rows, val = args if len(args) == 2 else (slice(None), args[0])
                out_ref[rows, hl * HEAD_DIM:(hl + 1) * HEAD_DIM] = val.astype(out_ref.dtype)
            return store

        hp_par = [_head_params(par_ref[head0 + hl], layer, depth) for hl in range(hp)]
        prep = [_hgrn_prepare(comp_of(hl), hp_par[hl][0], gstack_ref, nchunk) for hl in range(hp)]
        slots = 2 * hp + 1
        share = [n_pieces // slots + (i < n_pieces % slots) for i in range(slots)]
        project_some(share[0])
        ops = []
        for hl in range(hp):
            ops.append(_hgrn_operands(prep[hl], comp_of(hl), nchunk))
            project_some(share[1 + hl])
        scores = [_hgrn_products(ops[hl], lmask_ref) for hl in range(hp)]
        outs = []
        for hl in range(hp):
            o_c, st = _hgrn_outputs(scores[hl], st_ref[head0 + hl])
            st_ref[head0 + hl] = st
            outs.append(o_c)
            project_some(share[1 + hp + hl])
        for hl in range(hp):
            _hgrn_finish(outs[hl], comp_of(hl), hp_par[hl][1], store_of(hl, oh_ref))
            carry_ref[head0 + hl] = _short_conv(comp_of(hl), hp_par[hl][2], hp_par[hl][3],
                                                carry_ref[head0 + hl], store_of(hl, oc_ref))

    for parity, bufs in ((0, (pbuf_b, pbuf_a)), (1, (pbuf_a, pbuf_b))):
        for new_rows in (True, False):
            if new_rows and n_j % 2 == 0 and parity == 1:
                continue
            cond = jnp.logical_and(lax.rem(n, 2) == parity, (lax.rem(n, n_j) == 0) == new_rows)
            pl.when(cond)(functools.partial(step, *bufs, new_rows))


def _mixer(x2, g, w, par, layer, depth, consts, *, seq, tm, hp):
    m, d = x2.shape
    nh = par.shape[0]
    n_j = nh // hp
    n_r = seq // tm
    tn = hp * N_COMP * HEAD_DIM
    steps = (m // tm) * n_j

    def consumed(n):
        return jnp.maximum(n - 1, 0)

    def produced(n):
        return jnp.minimum(n, steps - 1)

    w_specs = [pl.BlockSpec((None, d, hp * HEAD_DIM),
                            lambda n, k=k: (layer, 0, k * n_j + lax.rem(produced(n), n_j)))
               for k in range(N_COMP)]
    out_sd = jax.ShapeDtypeStruct((m, nh * HEAD_DIM), BF16)
    out_spec = pl.BlockSpec((tm, hp * HEAD_DIM),
                            lambda n: (consumed(n) // n_j, lax.rem(consumed(n), n_j)))
    return pl.pallas_call(
        functools.partial(_mixer_kernel, layer, depth, hp, n_j, n_r),
        out_shape=(out_sd, out_sd),
        grid=(steps + 1,),
        in_specs=[pl.BlockSpec((tm, d), lambda n: (produced(n) // n_j, 0)), _resident(g.shape)]
        + w_specs + [_resident(par.shape)] + [_resident(c.shape) for c in consts],
        out_specs=(out_spec, out_spec),
        scratch_shapes=[pltpu.VMEM((tm, d), BF16), pltpu.VMEM((tm, tn), F32), pltpu.VMEM((tm, tn), F32),
                        pltpu.VMEM((nh, HEAD_DIM, HEAD_DIM), F32), pltpu.VMEM((nh, 8, HEAD_DIM), F32)],
        compiler_params=_params("arbitrary"),
        name="mixer",
    )(x2, g, *([w] * N_COMP), par, *consts)


def _mixout_kernel(oh_ref, oc_ref, x_ref, w_ref, g_ref, o_ref):
    kh = oh_ref.shape[1]
    mix = jnp.dot(oh_ref[...], w_ref[0:kh, :], preferred_element_type=F32)
    mix = mix + jnp.dot(oc_ref[...], w_ref[kh:, :], preferred_element_type=F32)
    o_ref[...] = x_ref[...] + _rms(mix, g_ref[...])


def _mixout(oh, oc, x2, w, g, layer, *, tm):
    m, d = x2.shape
    return pl.pallas_call(
        _mixout_kernel,
        out_shape=jax.ShapeDtypeStruct((m, d), F32),
        grid=(m // tm,),
        in_specs=[pl.BlockSpec((tm, oh.shape[1]), lambda i: (i, 0)),
                  pl.BlockSpec((tm, oc.shape[1]), lambda i: (i, 0)),
                  pl.BlockSpec((tm, d), lambda i: (i, 0)),
                  _layer_resident(w.shape, layer), _resident(g.shape)],
        out_specs=pl.BlockSpec((tm, d), lambda i: (i, 0)),
        compiler_params=_params("arbitrary"),
        name="mixout",
    )(oh, oc, x2, w, g)


def _kv_kernel(m_ref, g_ref, w_ref, o_ref):
    mn = _rms(m_ref[...], g_ref[...]).astype(BF16)
    o_ref[...] = jnp.dot(mn, w_ref[...], preferred_element_type=F32).astype(o_ref.dtype)


def _kvproj(mem2, g, wkv, layer, *, tm):
    m, d = mem2.shape
    return pl.pallas_call(
        _kv_kernel,
        out_shape=jax.ShapeDtypeStruct((2, m, d), BF16),
        grid=(2, m // tm),
        in_specs=[pl.BlockSpec((tm, d), lambda s, i: (i, 0)),
                  pl.BlockSpec((1, d), lambda s, i: (0, 0)),
                  pl.BlockSpec((None, None, d, d), lambda s, i: (layer, s, 0, 0))],
        out_specs=pl.BlockSpec((None, tm, d), lambda s, i: (s, i, 0)),
        compiler_params=_params("arbitrary", "arbitrary"),
        name="kvproj",
    )(mem2, g, wkv)


def _xattn_kernel(x_ref, gpre_ref, wq_ref, k_ref, v_ref, wo_ref, gpost_ref, o_ref):
    x = x_ref[...]
    d = x.shape[1]
    hd = d // XATTN_HEADS
    q = jnp.dot(_rms(x, gpre_ref[...]).astype(BF16), wq_ref[...], preferred_element_type=F32)
    heads = []
    for h in range(XATTN_HEADS):
        cols = slice(h * hd, (h + 1) * hd)
        s = lax.dot_general(q[:, cols].astype(BF16), k_ref[:, cols], NT_DIMS,
                            preferred_element_type=F32) * (hd ** -0.5)
        p = jnp.exp(s - jnp.max(s, axis=-1, keepdims=True))
        p = p / jnp.sum(p, axis=-1, keepdims=True)
        heads.append(jnp.dot(p.astype(BF16), v_ref[:, cols],
                             preferred_element_type=F32).astype(BF16))
    a = jnp.dot(jnp.concatenate(heads, axis=1), wo_ref[...], preferred_element_type=F32)
    o_ref[...] = x + _rms(a, gpost_ref[...])


def _xattn(x3, gpre, wq, kv, wo, gpost, layer, *, tq):
    bsz, seq, d = x3.shape
    nmem = kv.shape[2]
    return pl.pallas_call(
        _xattn_kernel,
        out_shape=jax.ShapeDtypeStruct((bsz, seq, d), F32),
        grid=(bsz, seq // tq),
        in_specs=[pl.BlockSpec((None, tq, d), lambda b, t: (b, t, 0)),
                  _resident(gpre.shape), _layer_resident(wq.shape, layer),
                  pl.BlockSpec((None, None, nmem, d), lambda b, t: (0, b, 0, 0)),
                  pl.BlockSpec((None, None, nmem, d), lambda b, t: (1, b, 0, 0)),
                  _layer_resident(wo.shape, layer), _resident(gpost.shape)],
        out_specs=pl.BlockSpec((None, tq, d), lambda b, t: (b, t, 0)),
        compiler_params=_params("arbitrary", "arbitrary"),
        name="xattn",
    )(x3, gpre, wq, kv, kv, wo, gpost)


def _mlp_kernel(x_ref, gpre_ref, wu_ref, wd_ref, gpost_ref, o_ref, hn_ref):
    j = pl.program_id(1)
    last = pl.num_programs(1) - 1
    half = o_ref.shape[0] // 2
    halves = (slice(0, half), slice(half, 2 * half))

    def up(rows):
        u = jnp.maximum(jnp.dot(hn_ref[rows, :], wu_ref[...], preferred_element_type=F32), 0.0)
        return (u * u).astype(BF16)

    @pl.when(j == 0)
    def _():
        for rows in halves:
            hn_ref[rows, :] = _rms(x_ref[rows, :], gpre_ref[...]).astype(BF16)
        uu = [up(rows) for rows in halves]
        for rows, u in zip(halves, uu):
            o_ref[rows, :] = jnp.dot(u, wd_ref[...], preferred_element_type=F32)

    @pl.when(jnp.logical_and(j > 0, j < last))
    def _():
        o_ref[...] += jnp.dot(up(slice(None)), wd_ref[...], preferred_element_type=F32)

    @pl.when(j == last)
    def _():
        uu = [up(rows) for rows in halves]
        for rows, u in zip(halves, uu):
            acc = o_ref[rows, :] + jnp.dot(u, wd_ref[...], preferred_element_type=F32)
            o_ref[rows, :] = x_ref[rows, :] + _rms(acc, gpost_ref[...])


def _mlp(x2, gpre, wu, wd, gpost, layer, *, tm, tf):
    m, d = x2.shape
    ff = wu.shape[2]
    assert ff // tf >= 2, "the kernel's first and last hidden blocks must be different steps"
    return pl.pallas_call(
        _mlp_kernel,
        out_shape=jax.ShapeDtypeStruct((m, d), F32),
        grid=(m // tm, ff // tf),
        in_specs=[pl.BlockSpec((tm, d), lambda i, j: (i, 0)),
                  _resident(gpre.shape),
                  pl.BlockSpec((None, d, tf), lambda i, j: (layer, 0, j)),
                  pl.BlockSpec((None, tf, d), lambda i, j: (layer, j, 0)),
                  _resident(gpost.shape)],
        out_specs=pl.BlockSpec((tm, d), lambda i, j: (i, 0)),
        scratch_shapes=[pltpu.VMEM((tm, d), BF16)],
        compiler_params=_params("arbitrary", "arbitrary"),
        name="mlp",
    )(x2, gpre, wu, wd, gpost)


def _tile(n, want):
    t = min(n, want)
    while n % t:
        t //= 2
    return t


def _tiles(m, seq, n_mem_rows, n_heads, d_ff):
    return dict(
        mixer=dict(tm=_tile(seq, 512), hp=_tile(n_heads, 2)),
        mixout=dict(tm=_tile(m, 512)),
        kvproj=dict(tm=_tile(n_mem_rows, 1024)),
        xattn=dict(tq=_tile(seq, 512)),
        mlp=dict(tm=_tile(m, 512), tf=_tile(d_ff, 2048)),
    )


def kernel(x, mem, g_mix_pre, w_in, hgrn_lb_logits, hgrn_norm_g, conv_w, conv_norm_g, w_mix_out,
           g_mix_post, g_x_pre, g_mem, w_q, w_k, w_v, w_xo, g_x_post, g_mlp_pre, w_up, w_down,
           g_mlp_post):
    bsz, seq, d = x.shape
    depth = w_in.shape[0]
    nmem = mem.shape[1]
    m = bsz * seq
    nh = hgrn_norm_g.shape[1] // HEAD_DIM
    assert w_in.shape[2] == N_COMP * nh * HEAD_DIM and conv_norm_g.shape[1] == nh * HEAD_DIM

    w_in_b = w_in.astype(BF16)
    w_mix_b = w_mix_out.astype(BF16)
    w_q_b = w_q.astype(BF16)
    w_kv_b = jnp.stack([w_k, w_v], axis=1).astype(BF16)
    w_xo_b = w_xo.astype(BF16)
    w_up_b = w_up.astype(BF16)
    w_down_b = w_down.astype(BF16)
    consts = _hgrn_constants()

    def row(p, l):
        return p[l][None, :]

    t = _tiles(m, seq, bsz * nmem, nh, w_up.shape[2])
    mem2 = mem.reshape(bsz * nmem, d)
    x2 = x.reshape(m, d)
    for l in range(depth):
        par = jnp.concatenate([hgrn_lb_logits, row(hgrn_norm_g, l), conv_w[l], row(conv_norm_g, l)],
                              axis=0)
        par = par.reshape(par.shape[0], nh, HEAD_DIM).transpose(1, 0, 2)
        oh, oc = _mixer(x2, row(g_mix_pre, l), w_in_b, par, l, depth, consts, seq=seq, **t["mixer"])
        x2 = _mixout(oh, oc, x2, w_mix_b, row(g_mix_post, l), l, **t["mixout"])
        kv = _kvproj(mem2, row(g_mem, l), w_kv_b, l, **t["kvproj"])
        x3 = _xattn(x2.reshape(bsz, seq, d), row(g_x_pre, l), w_q_b,
                    kv.reshape(2, bsz, nmem, d), w_xo_b, row(g_x_post, l), l, **t["xattn"])
        x2 = _mlp(x3.reshape(m, d), row(g_mlp_pre, l), w_up_b, w_down_b, row(g_mlp_post, l), l,
                  **t["mlp"])
    return x2.reshape(bsz, seq, d)
```

```python
import functools

import numpy as np
import jax
import jax.numpy as jnp
from jax import lax
from jax.experimental import pallas as pl
from jax.experimental.pallas import tpu as pltpu

F32 = jnp.float32
BF16 = jnp.bfloat16

EPS = 1e-6
HEAD_DIM = 128
CHUNK = 64
DIAG = 8
LEVELS = tuple(h for h in (32, 16, 8) if h >= DIAG)
LOG_TERMS = 2
CONV_K = 3
XATTN_HEADS = 4
Q, F, I, G, CB, CC, CH = range(7)
N_COMP = 7
V7X_VMEM_LIMIT_BYTES = 60 * 1024 * 1024

NT_DIMS = (((1,), (1,)), ((), ()))
TN_DIMS = (((0,), (0,)), ((), ()))


def _params(*semantics):
    return pltpu.CompilerParams(dimension_semantics=semantics,
                                vmem_limit_bytes=V7X_VMEM_LIMIT_BYTES)


def _rms(x, g):
    return x * lax.rsqrt(jnp.mean(x * x, axis=-1, keepdims=True) + EPS) * g


def _sigmoid(x):
    return 1.0 / (1.0 + jnp.exp(-x))


def _resident(shape):
    return pl.BlockSpec(shape, lambda *_: (0,) * len(shape), pipeline_mode=pl.Buffered(1))


def _layer_resident(shape, layer):
    return pl.BlockSpec((None,) + tuple(shape[1:]), lambda *_: (layer,) + (0,) * (len(shape) - 1),
                        pipeline_mode=pl.Buffered(1))


def _hgrn_constants():
    t = np.arange(CHUNK)
    mats = [t[None, :] <= t[:, None]]
    lmask = []
    for h in LEVELS:
        blk = t // h
        odd = (blk % 2 == 1)
        g = np.zeros((CHUNK, CHUNK), bool)
        for i in range(CHUNK):
            if odd[i]:
                g[i, blk[i] * h:i + 1] = True
            else:
                g[i, i + 1:(blk[i] + 1) * h] = True
        if h > DIAG:
            mats.append(g)
        lmask.append(odd[:, None] & (blk[None, :] == blk[:, None] - 1))
    gstack = np.concatenate(mats, axis=0).astype(np.float32)
    gstack = np.concatenate([gstack] * LOG_TERMS, axis=1)
    return jnp.asarray(gstack, BF16), jnp.asarray(np.stack(lmask), F32)


def _head_params(par, layer, depth):
    lg = [par[l:l + 1, :] for l in range(depth)]
    mx = functools.reduce(jnp.maximum, lg)
    ex = [jnp.exp(v - mx) for v in lg]
    den = functools.reduce(lambda a, b: a + b, ex)
    lb = jnp.zeros_like(mx)
    for l in range(1, layer + 1):
        lb = lb + ex[l] / den
    gh = par[depth:depth + 1, :]
    cw = [par[depth + 1 + j:depth + 2 + j, :] for j in range(CONV_K)]
    gc = par[depth + 1 + CONV_K:depth + 2 + CONV_K, :]
    return lb, gh, cw, gc


def _hgrn_prepare(comp, lb, gstack_ref, nchunk):
    qin = comp(Q)
    f = lb + (1.0 - lb) * _sigmoid(comp(F))
    lf = jnp.log2(f)
    kk_all = 1.0 - f
    lk_all = jnp.log2(jnp.maximum(kk_all, 0.0))
    qs_all = qin * _sigmoid(qin)
    terms, rest = [], lf
    for _ in range(LOG_TERMS):
        terms.append(rest.astype(BF16))
        rest = rest - terms[-1].astype(F32)
    rhs = jnp.concatenate(
        [jnp.concatenate([tm[c * CHUNK:(c + 1) * CHUNK] for tm in terms], axis=0)
         for c in range(nchunk)], axis=1)
    xs_all = jnp.dot(gstack_ref[...], rhs, preferred_element_type=F32)
    return qs_all, kk_all, lk_all, xs_all


def _hgrn_operands(prep, comp, nchunk):
    qs_all, kk_all, lk_all, xs_all = prep
    nblk = CHUNK // DIAG
    tl = lax.broadcasted_iota(jnp.int32, (DIAG, HEAD_DIM), 0)
    row = lax.broadcasted_iota(jnp.int32, (CHUNK, CHUNK), 0)
    dcol = lax.broadcasted_iota(jnp.int32, (CHUNK, CHUNK), 1) - (row - lax.rem(row, DIAG))
    qe_c, vb_c, dec_c, khat_c, w_c, ad_c = [], [], [], [], [], []
    for c in range(nchunk):
        rows = slice(c * CHUNK, (c + 1) * CHUNK)
        lanes = slice(c * HEAD_DIM, (c + 1) * HEAD_DIM)
        qs = qs_all[rows]
        kk = kk_all[rows]
        b = xs_all[0:CHUNK, lanes]
        bl = jnp.concatenate([b[0:DIAG]] + [b[r:r + DIAG] - b[r - 1:r]
                                             for r in range(DIAG, CHUNK, DIAG)], axis=0)

        blast = b[CHUNK - 1:CHUNK, :]
        qe_c.append((qs * jnp.exp2(b)).astype(BF16))
        khat_c.append((kk * jnp.exp2(blast - b)).astype(BF16))
        dec_c.append(jnp.exp2(blast))
        vb_c.append(comp(I, rows).astype(BF16))

        ws = []
        for li, h in enumerate(LEVELS):
            if h > DIAG:
                xl = xs_all[(1 + li) * CHUNK:(2 + li) * CHUNK, lanes]
            else:
                xl = jnp.concatenate([bl[r:r + h] if (r // h) % 2 else bl[r + h - 1:r + h] - bl[r:r + h]
                                      for r in range(0, CHUNK, h)], axis=0)
            base = jnp.concatenate([(qs if (r // h) % 2 else kk)[r:r + h]
                                    for r in range(0, CHUNK, h)], axis=0)
            ws.append((base * jnp.exp2(xl)).astype(BF16))
        w_c.append(ws)

        bk = bl - lk_all[rows]
        ad = jnp.zeros((CHUNK, CHUNK), F32)
        for s in range(DIAG):
            sums = []
            for blk in range(nblk):
                r = blk * DIAG
                p = jnp.exp2(bl[r:r + DIAG] - bk[r + s:r + s + 1]) * qs[r:r + DIAG]
                p = p if s == 0 else jnp.where(tl >= s, p, 0.0)
                sums.append(jnp.sum(p, axis=-1, keepdims=True))
            ad = jnp.where(dcol == s, jnp.concatenate(sums, axis=0), ad)
        ad_c.append(ad)
    return qe_c, vb_c, dec_c, khat_c, w_c, ad_c


def _hgrn_products(ops, lmask_ref):
    qe_c, vb_c, dec_c, khat_c, w_c, ad_c = ops
    upd_c, a_c = [], []
    for c in range(len(qe_c)):
        upd_c.append(lax.dot_general(vb_c[c], khat_c[c], TN_DIMS, preferred_element_type=F32))
        a = ad_c[c]
        for li, w in enumerate(w_c[c]):
            a = a + lax.dot_general(w, w, NT_DIMS, preferred_element_type=F32) * lmask_ref[li]
        a_c.append(a.astype(BF16))
    return qe_c, vb_c, dec_c, upd_c, a_c


def _hgrn_outputs(scores, st):
    qe_c, vb_c, dec_c, upd_c, a_c = scores
    o_c = []
    for c in range(len(qe_c)):
        o = lax.dot_general(qe_c[c], st.astype(BF16), NT_DIMS, preferred_element_type=F32)
        o_c.append(o + jnp.dot(a_c[c], vb_c[c], preferred_element_type=F32))
        st = st * dec_c[c] + upd_c[c]
    return o_c, st


def _hgrn_finish(o_c, comp, gh, store):
    for c, o in enumerate(o_c):
        rows = slice(c * CHUNK, (c + 1) * CHUNK)
        gin = comp(G, rows)
        y = o * lax.rsqrt(jnp.mean(o * o, axis=-1, keepdims=True) + EPS) * gh
        store(rows, y * (gin * _sigmoid(gin)))


def _short_conv(comp, cw, gc, carry, store):
    u = comp(CC) * comp(CH)
    rows_t = u.shape[0]
    tl = lax.broadcasted_iota(jnp.int32, (8, HEAD_DIM), 0)
    prev1 = carry[7:8, :]
    prev2 = carry[6:7, :]
    u1 = pltpu.roll(u, 1, axis=0)
    u2 = pltpu.roll(u, 2, axis=0)
    u1 = jnp.concatenate([jnp.where(tl == 0, prev1, u1[0:8]), u1[8:]], axis=0)
    u2 = jnp.concatenate([jnp.where(tl == 0, prev2, jnp.where(tl == 1, prev1, u2[0:8])), u2[8:]],
                         axis=0)
    z = comp(CB) * (cw[0] * u2 + cw[1] * u1 + cw[2] * u)
    store(z * lax.rsqrt(jnp.mean(z * z, axis=-1, keepdims=True) + EPS) * gc)
    return u[rows_t - 8:rows_t, :]


def _mixer_kernel(layer, depth, hp, n_j, n_r, x_ref, g_ref, w_ref, par_ref, gstack_ref, lmask_ref,
                  oh_ref, oc_ref, hn_ref, pbuf_a, pbuf_b, st_ref, carry_ref):
    n = pl.program_id(0)
    p = jnp.maximum(n - 1, 0)
    head0 = lax.rem(p, n_j) * hp
    col0 = lax.rem(jnp.minimum(n, pl.num_programs(0) - 2), n_j) * (hp * HEAD_DIM)
    rows_t = hn_ref.shape[0]
    nchunk = rows_t // CHUNK
    n_pieces = N_COMP

    @pl.when(n == 0)
    def _():
        pbuf_b[...] = jnp.zeros_like(pbuf_b)

    @pl.when(lax.rem(p // n_j, n_r) == 0)
    def _():
        for hl in range(hp):
            st_ref[head0 + hl] = jnp.zeros((HEAD_DIM, HEAD_DIM), F32)
            carry_ref[head0 + hl] = jnp.zeros((8, HEAD_DIM), F32)

    def step(src_ref, dst_ref, new_rows):
        if new_rows:
            hn_ref[...] = _rms(x_ref[...], g_ref[...]).astype(BF16)
        pieces = iter(range(n_pieces))

        def project_some(count):
            for k in (next(pieces, None) for _ in range(count)):
                if k is not None:
                    cols = pl.ds(pl.multiple_of(k * (n_j * hp * HEAD_DIM) + col0, hp * HEAD_DIM),
                                 hp * HEAD_DIM)
                    res = jnp.dot(hn_ref[...], w_ref[:, cols], preferred_element_type=F32)
                    for hl in range(hp):
                        c0 = (hl * N_COMP + k) * HEAD_DIM
                        dst_ref[:, c0:c0 + HEAD_DIM] = res[:, hl * HEAD_DIM:(hl + 1) * HEAD_DIM]

        def comp_of(hl):
            def comp(k, rows=slice(None)):
                c0 = (hl * N_COMP + k) * HEAD_DIM
                return src_ref[rows, c0:c0 + HEAD_DIM]
            return comp

        def store_of(hl, out_ref):
            def store(*args):
                rows, val = args if len(args) == 2 else (slice(None), args[0])
                out_ref[rows, hl * HEAD_DIM:(hl + 1) * HEAD_DIM] = val.astype(out_ref.dtype)
            return store

        hp_par = [_head_params(par_ref[head0 + hl], layer, depth) for hl in range(hp)]
        prep = [_hgrn_prepare(comp_of(hl), hp_par[hl][0], gstack_ref, nchunk) for hl in range(hp)]
        slots = 2 * hp + 1
        share = [n_pieces // slots + (i < n_pieces % slots) for i in range(slots)]
        project_some(share[0])
        ops = []
        for hl in range(hp):
            ops.append(_hgrn_operands(prep[hl], comp_of(hl), nchunk))
            project_some(share[1 + hl])
        scores = [_hgrn_products(ops[hl], lmask_ref) for hl in range(hp)]
        outs = []
        for hl in range(hp):
            o_c, st = _hgrn_outputs(scores[hl], st_ref[head0 + hl])
            st_ref[head0 + hl] = st
            outs.append(o_c)
            project_some(share[1 + hp + hl])
        for hl in range(hp):
            _hgrn_finish(outs[hl], comp_of(hl), hp_par[hl][1], store_of(hl, oh_ref))
            carry_ref[head0 + hl] = _short_conv(comp_of(hl), hp_par[hl][2], hp_par[hl][3],
                                                carry_ref[head0 + hl], store_of(hl, oc_ref))

    for parity, bufs in ((0, (pbuf_b, pbuf_a)), (1, (pbuf_a, pbuf_b))):
        for new_rows in (True, False):
            if new_rows and n_j % 2 == 0 and parity == 1:
                continue
            cond = jnp.logical_and(lax.rem(n, 2) == parity, (lax.rem(n, n_j) == 0) == new_rows)
            pl.when(cond)(functools.partial(step, *bufs, new_rows))


def _mixer(x2, g, w, par, layer, depth, consts, *, seq, tm, hp):
    m, d = x2.shape
    nh = par.shape[0]
    n_j = nh // hp
    n_r = seq // tm
    tn = hp * N_COMP * HEAD_DIM
    steps = (m // tm) * n_j

    def consumed(n):
        return jnp.maximum(n - 1, 0)

    def produced(n):
        return jnp.minimum(n, steps - 1)

    out_sd = jax.ShapeDtypeStruct((m, nh * HEAD_DIM), BF16)
    out_spec = pl.BlockSpec((tm, hp * HEAD_DIM),
                            lambda n: (consumed(n) // n_j, lax.rem(consumed(n), n_j)))
    return pl.pallas_call(
        functools.partial(_mixer_kernel, layer, depth, hp, n_j, n_r),
        out_shape=(out_sd, out_sd),
        grid=(steps + 1,),
        in_specs=[pl.BlockSpec((tm, d), lambda n: (produced(n) // n_j, 0)), _resident(g.shape),
                  _layer_resident(w.shape, layer), _resident(par.shape)]
        + [_resident(c.shape) for c in consts],
        out_specs=(out_spec, out_spec),
        scratch_shapes=[pltpu.VMEM((tm, d), BF16), pltpu.VMEM((tm, tn), F32), pltpu.VMEM((tm, tn), F32),
                        pltpu.VMEM((nh, HEAD_DIM, HEAD_DIM), F32), pltpu.VMEM((nh, 8, HEAD_DIM), F32)],
        compiler_params=_params("arbitrary"),
        name="mixer",
    )(x2, g, w, par, *consts)


def _mixout_kernel(oh_ref, oc_ref, x_ref, w_ref, g_ref, o_ref):
    kh = oh_ref.shape[1]
    mix = jnp.dot(oh_ref[...], w_ref[0:kh, :], preferred_element_type=F32)
    mix = mix + jnp.dot(oc_ref[...], w_ref[kh:, :], preferred_element_type=F32)
    o_ref[...] = x_ref[...] + _rms(mix, g_ref[...])


def _mixout(oh, oc, x2, w, g, layer, *, tm):
    m, d = x2.shape
    return pl.pallas_call(
        _mixout_kernel,
        out_shape=jax.ShapeDtypeStruct((m, d), F32),
        grid=(m // tm,),
        in_specs=[pl.BlockSpec((tm, oh.shape[1]), lambda i: (i, 0)),
                  pl.BlockSpec((tm, oc.shape[1]), lambda i: (i, 0)),
                  pl.BlockSpec((tm, d), lambda i: (i, 0)),
                  _layer_resident(w.shape, layer), _resident(g.shape)],
        out_specs=pl.BlockSpec((tm, d), lambda i: (i, 0)),
        compiler_params=_params("arbitrary"),
        name="mixout",
    )(oh, oc, x2, w, g)


def _kv_kernel(m_ref, g_ref, w_ref, o_ref):
    mn = _rms(m_ref[...], g_ref[...]).astype(BF16)
    o_ref[...] = jnp.dot(mn, w_ref[...], preferred_element_type=F32).astype(o_ref.dtype)


def _kvproj(mem2, g, wkv, layer, *, tm):
    m, d = mem2.shape
    return pl.pallas_call(
        _kv_kernel,
        out_shape=jax.ShapeDtypeStruct((2, m, d), BF16),
        grid=(2, m // tm),
        in_specs=[pl.BlockSpec((tm, d), lambda s, i: (i, 0)),
                  pl.BlockSpec((1, d), lambda s, i: (0, 0)),
                  pl.BlockSpec((None, None, d, d), lambda s, i: (layer, s, 0, 0))],
        out_specs=pl.BlockSpec((None, tm, d), lambda s, i: (s, i, 0)),
        compiler_params=_params("arbitrary", "arbitrary"),
        name="kvproj",
    )(mem2, g, wkv)


def _xattn_kernel(x_ref, gpre_ref, wq_ref, k_ref, v_ref, wo_ref, gpost_ref, o_ref):
    x = x_ref[...]
    d = x.shape[1]
    hd = d // XATTN_HEADS
    q = jnp.dot(_rms(x, gpre_ref[...]).astype(BF16), wq_ref[...], preferred_element_type=F32)
    heads = []
    for h in range(XATTN_HEADS):
        cols = slice(h * hd, (h + 1) * hd)
        s = lax.dot_general(q[:, cols].astype(BF16), k_ref[:, cols], NT_DIMS,
                            preferred_element_type=F32) * (hd ** -0.5)
        p = jnp.exp(s - jnp.max(s, axis=-1, keepdims=True))
        p = p / jnp.sum(p, axis=-1, keepdims=True)
        heads.append(jnp.dot(p.astype(BF16), v_ref[:, cols],
                             preferred_element_type=F32).astype(BF16))
    a = jnp.dot(jnp.concatenate(heads, axis=1), wo_ref[...], preferred_element_type=F32)
    o_ref[...] = x + _rms(a, gpost_ref[...])


def _xattn(x3, gpre, wq, kv, wo, gpost, layer, *, tq):
    bsz, seq, d = x3.shape
    nmem = kv.shape[2]
    return pl.pallas_call(
        _xattn_kernel,
        out_shape=jax.ShapeDtypeStruct((bsz, seq, d), F32),
        grid=(bsz, seq // tq),
        in_specs=[pl.BlockSpec((None, tq, d), lambda b, t: (b, t, 0)),
                  _resident(gpre.shape), _layer_resident(wq.shape, layer),
                  pl.BlockSpec((None, None, nmem, d), lambda b, t: (0, b, 0, 0)),
                  pl.BlockSpec((None, None, nmem, d), lambda b, t: (1, b, 0, 0)),
                  _layer_resident(wo.shape, layer), _resident(gpost.shape)],
        out_specs=pl.BlockSpec((None, tq, d), lambda b, t: (b, t, 0)),
        compiler_params=_params("arbitrary", "arbitrary"),
        name="xattn",
    )(x3, gpre, wq, kv, kv, wo, gpost)


def _mlp_kernel(x_ref, gpre_ref, wu_ref, wd_ref, gpost_ref, o_ref, hn_ref):
    j = pl.program_id(1)
    last = pl.num_programs(1) - 1
    half = o_ref.shape[0] // 2
    halves = (slice(0, half), slice(half, 2 * half))

    def up(rows):
        u = jnp.maximum(jnp.dot(hn_ref[rows, :], wu_ref[...], preferred_element_type=F32), 0.0)
        return (u * u).astype(BF16)

    @pl.when(j == 0)
    def _():
        for rows in halves:
            hn_ref[rows, :] = _rms(x_ref[rows, :], gpre_ref[...]).astype(BF16)
        uu = [up(rows) for rows in halves]
        for rows, u in zip(halves, uu):
            o_ref[rows, :] = jnp.dot(u, wd_ref[...], preferred_element_type=F32)

    @pl.when(jnp.logical_and(j > 0, j < last))
    def _():
        o_ref[...] += jnp.dot(up(slice(None)), wd_ref[...], preferred_element_type=F32)

    @pl.when(j == last)
    def _():
        uu = [up(rows) for rows in halves]
        for rows, u in zip(halves, uu):
            acc = o_ref[rows, :] + jnp.dot(u, wd_ref[...], preferred_element_type=F32)
            o_ref[rows, :] = x_ref[rows, :] + _rms(acc, gpost_ref[...])


def _mlp(x2, gpre, wu, wd, gpost, layer, *, tm, tf):
    m, d = x2.shape
    ff = wu.shape[2]
    assert ff // tf >= 2, "the kernel's first and last hidden blocks must be different steps"
    return pl.pallas_call(
        _mlp_kernel,
        out_shape=jax.ShapeDtypeStruct((m, d), F32),
        grid=(m // tm, ff // tf),
        in_specs=[pl.BlockSpec((tm, d), lambda i, j: (i, 0)),
                  _resident(gpre.shape),
                  pl.BlockSpec((None, d, tf), lambda i, j: (layer, 0, j)),
                  pl.BlockSpec((None, tf, d), lambda i, j: (layer, j, 0)),
                  _resident(gpost.shape)],
        out_specs=pl.BlockSpec((tm, d), lambda i, j: (i, 0)),
        scratch_shapes=[pltpu.VMEM((tm, d), BF16)],
        compiler_params=_params("arbitrary", "arbitrary"),
        name="mlp",
    )(x2, gpre, wu, wd, gpost)


def _tile(n, want):
    t = min(n, want)
    while n % t:
        t //= 2
    return t


def _tiles(m, seq, n_mem_rows, n_heads, d_ff):
    return dict(
        mixer=dict(tm=_tile(seq, 512), hp=_tile(n_heads, 2)),
        mixout=dict(tm=_tile(m, 512)),
        kvproj=dict(tm=_tile(n_mem_rows, 1024)),
        xattn=dict(tq=_tile(seq, 512)),
        mlp=dict(tm=_tile(m, 512), tf=_tile(d_ff, 2048)),
    )


def kernel(x, mem, g_mix_pre, w_in, hgrn_lb_logits, hgrn_norm_g, conv_w, conv_norm_g, w_mix_out,
           g_mix_post, g_x_pre, g_mem, w_q, w_k, w_v, w_xo, g_x_post, g_mlp_pre, w_up, w_down,
           g_mlp_post):
    bsz, seq, d = x.shape
    depth = w_in.shape[0]
    nmem = mem.shape[1]
    m = bsz * seq
    nh = hgrn_norm_g.shape[1] // HEAD_DIM
    assert w_in.shape[2] == N_COMP * nh * HEAD_DIM and conv_norm_g.shape[1] == nh * HEAD_DIM

    w_in_b = w_in.astype(BF16)
    w_mix_b = w_mix_out.astype(BF16)
    w_q_b = w_q.astype(BF16)
    w_kv_b = jnp.stack([w_k, w_v], axis=1).astype(BF16)
    w_xo_b = w_xo.astype(BF16)
    w_up_b = w_up.astype(BF16)
    w_down_b = w_down.astype(BF16)
    consts = _hgrn_constants()

    def row(p, l):
        return p[l][None, :]

    t = _tiles(m, seq, bsz * nmem, nh, w_up.shape[2])
    mem2 = mem.reshape(bsz * nmem, d)
    x2 = x.reshape(m, d)
    for l in range(depth):
        par = jnp.concatenate([hgrn_lb_logits, row(hgrn_norm_g, l), conv_w[l], row(conv_norm_g, l)],
                              axis=0)
        par = par.reshape(par.shape[0], nh, HEAD_DIM).transpose(1, 0, 2)
        oh, oc = _mixer(x2, row(g_mix_pre, l), w_in_b, par, l, depth, consts, seq=seq, **t["mixer"])
        x2 = _mixout(oh, oc, x2, w_mix_b, row(g_mix_post, l), l, **t["mixout"])
        kv = _kvproj(mem2, row(g_mem, l), w_kv_b, l, **t["kvproj"])
        x3 = _xattn(x2.reshape(bsz, seq, d), row(g_x_pre, l), w_q_b,
                    kv.reshape(2, bsz, nmem, d), w_xo_b, row(g_x_post, l), l, **t["xattn"])
        x2 = _mlp(x3.reshape(m, d), row(g_mlp_pre, l), w_up_b, w_down_b, row(g_mlp_post, l), l,
                  **t["mlp"])
    return x2.reshape(bsz, seq, d)
```
